```python
import math
import jax, jax.numpy as jnp
from jax import lax
import numpy as np

D_MODEL = 1024
BATCH = 4
SEQ = 4096
DEPTH = 2
DEC_BATCH = 2
DEC_SEQ = 16384
PAST_LEN = 128

D_MIX = D_MODEL
D_SSD = D_MIX // 2
D_ATT = D_MIX - D_SSD
SSD_HEAD_DIM = 64
H_SSD = D_SSD // SSD_HEAD_DIM
SSD_GROUPS = 2
D_STATE = 64
CONV_W = 5
CONV_DIM = D_SSD + 2 * SSD_GROUPS * D_STATE
CHUNK = 128
ATT_HEAD_DIM = 64
H_ATT = D_ATT // ATT_HEAD_DIM
GRID_W = 64
KH_MAX = 8
KW = 16
N_EXPERTS = 16
CAP_FACTOR = 2
D_FF_EXPERT = 2 * D_MODEL
RMS_EPS = 1e-6
IN_COLS = D_SSD + CONV_DIM + 2 * H_SSD + 3 * D_ATT

kernel_name = "hymba_ssd_natten_ec_encoder"


def rmsnorm(x, g):
    xf = x.astype(jnp.float32)
    y = xf * lax.rsqrt(jnp.mean(xf * xf, axis=-1, keepdims=True) + RMS_EPS)
    return (y * g.astype(jnp.float32)).astype(x.dtype)


def depthwise_conv_centred(u, w, b):
    y = lax.conv_general_dilated(u, w[:, None, :], window_strides=(1,),
                                 padding=[(CONV_W // 2, CONV_W // 2)],
                                 dimension_numbers=('NWC', 'WIO', 'NWC'),
                                 feature_group_count=u.shape[-1])
    return y + b


def ssd_one_direction(x, dt, a, bm, cm):
    b, l, h, p = x.shape
    n = bm.shape[-1]
    nc = l // CHUNK
    xdt = (x * dt[..., None]).reshape(b, nc, CHUNK, h, p)
    bc = bm.reshape(b, nc, CHUNK, h, n)
    cc = cm.reshape(b, nc, CHUNK, h, n)
    a_c = (dt * a).reshape(b, nc, CHUNK, h).transpose(0, 3, 1, 2)
    a_cs = jnp.cumsum(a_c, axis=-1)
    seg = a_cs[..., :, None] - a_cs[..., None, :]
    lower = jnp.tril(jnp.ones((CHUNK, CHUNK), dtype=bool))
    decay_in = jnp.exp(jnp.where(lower, seg, -jnp.inf))
    scores = jnp.einsum('bclhn,bcshn->bhcls', cc, bc) * decay_in
    y_diag = jnp.einsum('bhcls,bcshp->bclhp', scores, xdt)
    decay_to_end = jnp.exp(a_cs[..., -1:] - a_cs)
    chunk_states = jnp.einsum('bclhn,bhcl,bclhp->bchpn', bc, decay_to_end, xdt)
    chunk_decay = jnp.exp(a_cs[..., -1])

    def step(s, inp):
        dec, st = inp
        return dec[..., None, None] * s + st, s

    s0 = jnp.zeros((b, h, p, n), jnp.float32)
    _, prev = lax.scan(step, s0, (jnp.moveaxis(chunk_decay, -1, 0),
                                  jnp.moveaxis(chunk_states, 1, 0)))
    y_off = jnp.einsum('bclhn,cbhpn,bhcl->bclhp', cc, prev, jnp.exp(a_cs))
    return (y_diag + y_off).reshape(b, l, h, p)


def bidirectional_ssd(xs, dt_raw, bm, cm, dt_bias, a_log, d_skip):
    dtype = xs.dtype
    b, l, _ = xs.shape
    f32 = jnp.float32
    rep = H_SSD // SSD_GROUPS
    x = xs.astype(f32).reshape(b, l, H_SSD, SSD_HEAD_DIM)
    bh = jnp.repeat(bm.astype(f32).reshape(b, l, SSD_GROUPS, D_STATE), rep, axis=2)
    ch = jnp.repeat(cm.astype(f32).reshape(b, l, SSD_GROUPS, D_STATE), rep, axis=2)
    dtr = dt_raw.astype(f32)
    dtb = dt_bias.astype(f32)
    dt_f = jax.nn.softplus(dtr[..., :H_SSD] + dtb[0])
    dt_b = jax.nn.softplus(dtr[..., H_SSD:] + dtb[1])
    a = -jnp.exp(a_log.astype(f32))
    flip = lambda t: jnp.flip(t, axis=1)
    y_f = ssd_one_direction(x, dt_f, a[0], bh, ch)
    y_b = flip(ssd_one_direction(flip(x), flip(dt_b), a[1], flip(bh), flip(ch)))
    y = y_f + y_b + d_skip.astype(f32)[:, None] * x
    return y.reshape(b, l, D_SSD).astype(dtype)


def neighbourhood_attention(q, k, v, rpb):
    b, l, h, d = q.shape
    rows = l // GRID_W
    kh = min(KH_MAX, rows)
    r = jnp.arange(rows)
    row_start = jnp.clip(r - kh // 2, 0, rows - kh)
    row_idx = row_start[:, None] + jnp.arange(kh)
    qg = q.reshape(b, rows, GRID_W, h, d)
    kg = k.reshape(b, rows, GRID_W, h, d)[:, row_idx]
    vg = v.reshape(b, rows, GRID_W, h, d)[:, row_idx]
    c = jnp.arange(GRID_W)
    col_start = jnp.clip(c - KW // 2, 0, GRID_W - KW)
    col_mask = (c[None, :] >= col_start[:, None]) & (c[None, :] < col_start[:, None] + KW)
    dr_idx = row_idx - r[:, None] + (KH_MAX - 1)
    dc_idx = jnp.clip(c[None, :] - c[:, None] + (KW - 1), 0, 2 * KW - 2)
    bias = rpb[:, dr_idx[:, None, :, None], dc_idx[None, :, None, :]]
    s = jnp.einsum('brqhd,brkwhd->bhrqkw', qg, kg).astype(jnp.float32) * (d ** -0.5)
    s = s + bias.astype(jnp.float32)[None]
    s = jnp.where(col_mask[None, None, None, :, None, :], s, -jnp.inf)
    p = jax.nn.softmax(s.reshape(b, h, rows, GRID_W, kh * GRID_W), axis=-1)
    p = p.reshape(b, h, rows, GRID_W, kh, GRID_W).astype(v.dtype)
    o = jnp.einsum('bhrqkw,brkwhd->brqhd', p, vg)
    return o.reshape(b, l, h * d)


def expert_choice_ffn(u, w_router, w_gate, w_up, w_down):
    b, l, d = u.shape
    t = b * l
    uf = u.reshape(t, d)
    cap = CAP_FACTOR * t // N_EXPERTS
    aff = jax.nn.softmax(jnp.matmul(uf, w_router).astype(jnp.float32), axis=-1)
    gates, idx = lax.top_k(aff.T, cap)
    xe = uf[idx]
    hid = jax.nn.silu(jnp.einsum('ecd,edf->ecf', xe, w_gate)) * jnp.einsum('ecd,edf->ecf', xe, w_up)
    out = jnp.einsum('ecf,efd->ecd', hid, w_down) * gates[..., None].astype(u.dtype)
    y = jnp.zeros((t, d), u.dtype).at[idx.reshape(-1)].add(out.reshape(-1, d))
    return y.reshape(b, l, d)


def encoder_layer(x, norm_mix_g, w_in, conv_w, conv_b, dt_bias, a_log, d_skip,
                  ssd_norm_g, attn_norm_g, rpb, w_out, norm_ffn_g, w_router,
                  w_gate, w_up, w_down):
    b, l, _ = x.shape
    u = rmsnorm(x, norm_mix_g)
    proj = jnp.matmul(u, w_in)
    o1 = D_SSD
    o2 = o1 + CONV_DIM
    o3 = o2 + 2 * H_SSD
    o4 = o3 + D_ATT
    o5 = o4 + D_ATT
    z = proj[..., :o1]
    xbc = jax.nn.silu(depthwise_conv_centred(proj[..., o1:o2], conv_w, conv_b))
    dt_raw = proj[..., o2:o3]
    q = proj[..., o3:o4].reshape(b, l, H_ATT, ATT_HEAD_DIM)
    k = proj[..., o4:o5].reshape(b, l, H_ATT, ATT_HEAD_DIM)
    v = proj[..., o5:].reshape(b, l, H_ATT, ATT_HEAD_DIM)
    xs = xbc[..., :D_SSD]
    bm = xbc[..., D_SSD:D_SSD + SSD_GROUPS * D_STATE]
    cm = xbc[..., D_SSD + SSD_GROUPS * D_STATE:]
    y_ssd = bidirectional_ssd(xs, dt_raw, bm, cm, dt_bias, a_log, d_skip)
    y_ssd = rmsnorm(y_ssd * jax.nn.silu(z), ssd_norm_g)
    y_att = rmsnorm(neighbourhood_attention(q, k, v, rpb), attn_norm_g)
    x = x + jnp.matmul(jnp.concatenate([y_ssd, y_att], axis=-1), w_out)
    x = x + expert_choice_ffn(rmsnorm(x, norm_ffn_g), w_router, w_gate, w_up, w_down)
    return x


def run_trunk(x, norm_mix_g, w_in, conv_w, conv_b, dt_bias, a_log, d_skip,
              ssd_norm_g, attn_norm_g, rpb, w_out, norm_ffn_g, w_router,
              w_gate, w_up, w_down, norm_final_g):
    for i in range(DEPTH):
        x = encoder_layer(x, norm_mix_g[i], w_in[i], conv_w[i], conv_b[i], dt_bias[i],
                          a_log[i], d_skip[i], ssd_norm_g[i], attn_norm_g[i], rpb[i],
                          w_out[i], norm_ffn_g[i], w_router[i], w_gate[i], w_up[i],
                          w_down[i])
    return rmsnorm(x, norm_final_g)


def setup_inputs(seed: int = 0) -> dict:
    key = jax.random.key(seed)
    ks = jax.random.split(key, 20)
    f32 = jnp.float32
    nrm = lambda k, shape, s: jax.random.normal(k, shape, f32) * s
    dt0 = jnp.exp(jax.random.uniform(ks[6], (DEPTH, 2, H_SSD), f32,
                                     minval=math.log(1e-3), maxval=math.log(1e-1)))
    dt_bias = dt0 + jnp.log(-jnp.expm1(-dt0))
    a_log = jnp.log(jax.random.uniform(ks[7], (DEPTH, 2, H_SSD), f32, minval=1.0, maxval=16.0))
    return {
        "x_prompt": nrm(ks[0], (BATCH, SEQ, D_MODEL), 1.0),
        "x_sample": nrm(ks[1], (DEC_BATCH, DEC_SEQ, D_MODEL), 1.0),
        "norm_mix_g": 1.0 + nrm(ks[2], (DEPTH, D_MODEL), 0.01),
        "w_in": nrm(ks[3], (DEPTH, D_MODEL, IN_COLS), D_MODEL ** -0.5),
        "conv_w": nrm(ks[4], (DEPTH, CONV_W, CONV_DIM), CONV_W ** -0.5),
        "conv_b": nrm(ks[5], (DEPTH, CONV_DIM), 0.01),
        "dt_bias": dt_bias,
        "a_log": a_log,
        "d_skip": 1.0 + nrm(ks[8], (DEPTH, H_SSD), 0.1),
        "ssd_norm_g": 1.0 + nrm(ks[9], (DEPTH, D_SSD), 0.01),
        "attn_norm_g": 1.0 + nrm(ks[10], (DEPTH, D_ATT), 0.01),
        "rpb": nrm(ks[11], (DEPTH, H_ATT, 2 * KH_MAX - 1, 2 * KW - 1), 0.02),
        "w_out": nrm(ks[12], (DEPTH, D_MIX, D_MODEL), D_MIX ** -0.5),
        "norm_ffn_g": 1.0 + nrm(ks[13], (DEPTH, D_MODEL), 0.01),
        "w_router": nrm(ks[14], (DEPTH, D_MODEL, N_EXPERTS), D_MODEL ** -0.5),
        "w_gate": nrm(ks[15], (DEPTH, N_EXPERTS, D_MODEL, D_FF_EXPERT), D_MODEL ** -0.5),
        "w_up": nrm(ks[16], (DEPTH, N_EXPERTS, D_MODEL, D_FF_EXPERT), D_MODEL ** -0.5),
        "w_down": nrm(ks[17], (DEPTH, N_EXPERTS, D_FF_EXPERT, D_MODEL), D_FF_EXPERT ** -0.5),
        "norm_final_g": 1.0 + nrm(ks[18], (D_MODEL,), 0.01),
    }


def reference(x_prompt, x_sample, norm_mix_g, w_in, conv_w, conv_b, dt_bias, a_log,
              d_skip, ssd_norm_g, attn_norm_g, rpb, w_out, norm_ffn_g, w_router,
              w_gate, w_up, w_down, norm_final_g):
    y_prompt = run_trunk(x_prompt, norm_mix_g, w_in, conv_w, conv_b, dt_bias, a_log,
                         d_skip, ssd_norm_g, attn_norm_g, rpb, w_out, norm_ffn_g,
                         w_router, w_gate, w_up, w_down, norm_final_g)
    y_sample = run_trunk(x_sample, norm_mix_g, w_in, conv_w, conv_b, dt_bias, a_log,
                         d_skip, ssd_norm_g, attn_norm_g, rpb, w_out, norm_ffn_g,
                         w_router, w_gate, w_up, w_down, norm_final_g)
    return (y_prompt, y_sample)
```

```python
import functools

import jax
import jax.numpy as jnp
from jax import lax
from jax.experimental import pallas as pl
from jax.experimental.pallas import tpu as pltpu

F32 = jnp.float32
BF16 = jnp.bfloat16
U32 = jnp.uint32
I32 = jnp.int32

D_MODEL = 1024
D_SSD = 512
D_ATT = 512
N_HEADS = 8
CONV_W = 5
CONV_DIM = 768
CHUNK = 128
GRID_W = 64
KH = 8
KW = 16
N_EXPERTS = 16
CAP_FACTOR = 2
D_FF = 2048
RMS_EPS = 1e-6
NEG = -1e30

LANES = 128
HALO = 16
IN_COLS_PADDED = 2944

ROW_TILE = 512
ATT_ROWS = 8
ATT_KROWS = 16
TOK_BLOCK = 256
SLOT_WIN = 64
SLOT_STEP = 56
SUBLANES = 8
FFN_TILE = 512
VMEM_LIMIT = 56 * 1024 * 1024


def _cparams(sem):
    return pltpu.CompilerParams(dimension_semantics=sem, vmem_limit_bytes=VMEM_LIMIT)


def _sigmoid(x):
    return 1.0 / (1.0 + jnp.exp(-x))


def _silu(x):
    return x * _sigmoid(x)


def _softplus(x):
    return jnp.maximum(x, 0.0) + jnp.log(1.0 + jnp.exp(-jnp.abs(x)))


def _rms(x, g):
    return x * lax.rsqrt(jnp.mean(x * x, axis=-1, keepdims=True) + RMS_EPS) * g


def _bits(x):
    return lax.bitcast_convert_type(x, U32)


def _pack_halves(x):
    lo = _bits(x[:, :512]) >> 16
    hi = _bits(x[:, 512:]) & jnp.uint32(0xFFFF0000)
    return hi | lo


def _unpack_halves(p):
    lo = lax.bitcast_convert_type(p << 16, F32)
    hi = lax.bitcast_convert_type(p & jnp.uint32(0xFFFF0000), F32)
    return jnp.concatenate([lo, hi], axis=1).astype(BF16)


def _in_proj_kernel(x_ref, g_ref, w_ref, z_ref, xbc_ref, q_ref, k_ref, v_ref, dt_ref):
    u = _rms(x_ref[...], g_ref[...]).astype(BF16)

    def proj(a, b):
        return jnp.dot(u, w_ref[:, a:b], preferred_element_type=F32)

    z_ref[...] = proj(0, 512).astype(BF16)
    xbc_ref[...] = proj(512, 1280).astype(BF16)
    q_ref[...] = (proj(1280, 1792) * (64 ** -0.5)).astype(BF16)
    k_ref[...] = proj(1792, 2304).astype(BF16)
    v_ref[...] = proj(2304, 2816).astype(BF16)
    dt_ref[...] = proj(2816, 2944)


def _in_proj(x2d, g, w):
    t = x2d.shape[0]
    tm = min(ROW_TILE, t)
    row = lambda n: pl.BlockSpec((tm, n), lambda i: (i, 0))
    full = lambda a: pl.BlockSpec(a.shape, lambda i: (0, 0))
    return pl.pallas_call(
        _in_proj_kernel,
        grid=(t // tm,),
        in_specs=[row(D_MODEL), full(g), full(w)],
        out_specs=[row(512), row(CONV_DIM), row(512), row(512), row(512), row(LANES)],
        out_shape=[jax.ShapeDtypeStruct((t, 512), BF16),
                   jax.ShapeDtypeStruct((t, CONV_DIM), BF16),
                   jax.ShapeDtypeStruct((t, 512), BF16),
                   jax.ShapeDtypeStruct((t, 512), BF16),
                   jax.ShapeDtypeStruct((t, 512), BF16),
                   jax.ShapeDtypeStruct((t, LANES), F32)],
        compiler_params=_cparams(("parallel",)),
    )(x2d, g, w)


def _conv_silu(xc_ref, xp_ref, xn_ref, cw_ref, cb_ref, c, nc):
    cur = xc_ref[0].astype(F32)
    prev = jnp.where(c > 0, xp_ref[0].astype(F32), 0.0)
    nxt = jnp.where(c < nc - 1, xn_ref[0].astype(F32), 0.0)
    xpad = jnp.concatenate([prev, cur, nxt], axis=0)
    acc = jnp.broadcast_to(cb_ref[...], (CHUNK, CONV_DIM))
    for k in range(CONV_W):
        start = HALO - CONV_W // 2 + k
        acc = acc + cw_ref[k:k + 1, :] * xpad[start:start + CHUNK, :]
    return _silu(acc)


def _decay_terms(dt_ref, dtb_ref, a_ref):
    dtv = _softplus(dt_ref[0] + dtb_ref[...])
    a = dtv * a_ref[...]
    li = lax.broadcasted_iota(I32, (CHUNK, CHUNK), 0)
    si = lax.broadcasted_iota(I32, (CHUNK, CHUNK), 1)
    tri = jnp.where(si <= li, 1.0, 0.0).astype(F32)
    cs = jnp.dot(tri, a, preferred_element_type=F32, precision=lax.Precision.HIGHEST)
    tot = cs[CHUNK - 1:CHUNK, :]
    return dtv, cs, cs - a, tot


def _pair_cols(x, h0, left):
    return jnp.where(left, x[:, h0:h0 + 1], x[:, h0 + 1:h0 + 2])


def _ssd_bwd_state_kernel(xc_ref, xp_ref, xn_ref, dt_ref, cw_ref, cb_ref, dtb_ref, a_ref,
                          sin_ref, sb_ref, *, nc):
    i = pl.program_id(1)
    c = nc - 1 - i

    @pl.when(i == 0)
    def _():
        sb_ref[...] = jnp.zeros_like(sb_ref)

    sin_ref[0, 0] = sb_ref[...]
    act = _conv_silu(xc_ref, xp_ref, xn_ref, cw_ref, cb_ref, c, nc)
    xs = act[:, :D_SSD]
    bt = act[:, D_SSD:D_SSD + LANES].T.astype(BF16)
    dtv, _, ex, tot = _decay_terms(dt_ref, dtb_ref, a_ref)
    wb = jnp.exp(ex) * dtv
    decb = jnp.exp(tot)
    left = lax.broadcasted_iota(I32, (1, LANES), 1) < 64
    for j in range(4):
        xw = (xs[:, LANES * j:LANES * (j + 1)] * _pair_cols(wb, 8 + 2 * j, left)).astype(BF16)
        st = jnp.dot(bt, xw, preferred_element_type=F32)
        sb_ref[j] = _pair_cols(decb, 8 + 2 * j, left) * sb_ref[j] + st


def _ssd_main_kernel(xc_ref, xp_ref, xn_ref, dt_ref, z_ref, sin_ref, cw_ref, cb_ref, dtb_ref,
                     a_ref, dskip_ref, g_ref, o_ref, sf_ref, y_ref, *, nc):
    c = pl.program_id(1)

    @pl.when(c == 0)
    def _():
        sf_ref[...] = jnp.zeros_like(sf_ref)

    act = _conv_silu(xc_ref, xp_ref, xn_ref, cw_ref, cb_ref, c, nc)
    xs = act[:, :D_SSD]
    bm = act[:, D_SSD:D_SSD + LANES]
    cm = act[:, D_SSD + LANES:]
    dtv, cs, ex, tot = _decay_terms(dt_ref, dtb_ref, a_ref)
    cs_t, ex_t, dt_t = cs.T, ex.T, dtv.T
    wf = jnp.exp(tot - cs) * dtv
    ef = jnp.exp(cs)
    eb = jnp.exp(tot - ex)
    decf = jnp.exp(tot)

    lane = lax.broadcasted_iota(I32, (1, LANES), 1)
    left = lane < 64
    li = lax.broadcasted_iota(I32, (CHUNK, CHUNK), 0)
    si = lax.broadcasted_iota(I32, (CHUNK, CHUNK), 1)
    bm_b = bm.astype(BF16)
    cm_b = cm.astype(BF16)
    bt = bm.T.astype(BF16)
    zero = jnp.zeros_like(bm_b)
    for grp in range(2):
        in_grp = left if grp == 0 else jnp.logical_not(left)
        b_g = jnp.where(in_grp, bm_b, zero)
        c_g = jnp.where(in_grp, cm_b, zero)
        cb_g = lax.dot_general(cm_b, b_g, (((1,), (1,)), ((), ())),
                               preferred_element_type=F32)
        for jj in range(2):
            j = 2 * grp + jj
            x_p = xs[:, LANES * j:LANES * (j + 1)]
            x_pb = x_p.astype(BF16)
            halves = []
            for sub in range(2):
                h = 2 * j + sub
                dtf = dt_t[h:h + 1, :]
                dtb = dt_t[8 + h:9 + h, :]
                arg = jnp.where(si <= li,
                                cs[:, h:h + 1] - cs_t[h:h + 1, :],
                                ex_t[8 + h:9 + h, :] - ex[:, 8 + h:9 + h])
                dsel = jnp.where(si < li, dtf, jnp.where(si > li, dtb, dtf + dtb))
                m = (cb_g * jnp.exp(arg) * dsel).astype(BF16)
                halves.append(jnp.dot(m, x_pb, preferred_element_type=F32))
            y = jnp.where(left, halves[0], halves[1])
            sf = sf_ref[j]
            sb = sin_ref[0, 0, j]
            y = y + jnp.dot(c_g, sf.astype(BF16), preferred_element_type=F32) \
                * _pair_cols(ef, 2 * j, left)
            y = y + jnp.dot(c_g, sb.astype(BF16), preferred_element_type=F32) \
                * _pair_cols(eb, 8 + 2 * j, left)
            y_ref[:, LANES * j:LANES * (j + 1)] = y + dskip_ref[:, LANES * j:LANES * (j + 1)] * x_p
            xw = (x_p * _pair_cols(wf, 2 * j, left)).astype(BF16)
            st = jnp.dot(bt, xw, preferred_element_type=F32)
            sf_ref[j] = _pair_cols(decf, 2 * j, left) * sf + st

    yz = y_ref[...] * _silu(z_ref[0].astype(F32))
    o_ref[0] = _rms(yz, g_ref[...]).astype(BF16)


def _ssd(xbc, dt, z, cw, cb, dtb, a_lane, dskip, g):
    b, l, _ = xbc.shape
    nc = l // CHUNK
    hb = CHUNK // HALO
    last_halo = l // HALO - 1

    def specs(chunk_of):
        cur = pl.BlockSpec((1, CHUNK, CONV_DIM), lambda bi, i: (bi, chunk_of(i), 0))
        prev = pl.BlockSpec((1, HALO, CONV_DIM),
                            lambda bi, i: (bi, jnp.maximum(chunk_of(i) * hb - 1, 0), 0))
        nxt = pl.BlockSpec((1, HALO, CONV_DIM),
                           lambda bi, i: (bi, jnp.minimum((chunk_of(i) + 1) * hb, last_halo), 0))
        dts = pl.BlockSpec((1, CHUNK, LANES), lambda bi, i: (bi, chunk_of(i), 0))
        return cur, prev, nxt, dts

    const = lambda a: pl.BlockSpec(a.shape, lambda bi, i: (0,) * a.ndim)

    rev = lambda i: nc - 1 - i
    cur, prev, nxt, dts = specs(rev)
    s_in = pl.pallas_call(
        functools.partial(_ssd_bwd_state_kernel, nc=nc),
        grid=(b, nc),
        in_specs=[cur, prev, nxt, dts, const(cw), const(cb), const(dtb), const(a_lane)],
        out_specs=pl.BlockSpec((1, 1, 4, LANES, LANES), lambda bi, i: (bi, rev(i), 0, 0, 0)),
        out_shape=jax.ShapeDtypeStruct((b, nc, 4, LANES, LANES), F32),
        scratch_shapes=[pltpu.VMEM((4, LANES, LANES), F32)],
        compiler_params=_cparams(("parallel", "arbitrary")),
    )(xbc, xbc, xbc, dt, cw, cb, dtb, a_lane)

    fwd = lambda i: i
    cur, prev, nxt, dts = specs(fwd)
    return pl.pallas_call(
        functools.partial(_ssd_main_kernel, nc=nc),
        grid=(b, nc),
        in_specs=[cur, prev, nxt, dts,
                  pl.BlockSpec((1, CHUNK, D_SSD), lambda bi, i: (bi, i, 0)),
                  pl.BlockSpec((1, 1, 4, LANES, LANES), lambda bi, i: (bi, i, 0, 0, 0)),
                  const(cw), const(cb), const(dtb), const(a_lane), const(dskip), const(g)],
        out_specs=pl.BlockSpec((1, CHUNK, D_SSD), lambda bi, i: (bi, i, 0)),
        out_shape=jax.ShapeDtypeStruct((b, l, D_SSD), BF16),
        scratch_shapes=[pltpu.VMEM((4, LANES, LANES), F32), pltpu.VMEM((CHUNK, D_SSD), F32)],
        compiler_params=_cparams(("parallel", "arbitrary")),
    )(xbc, xbc, xbc, dt, z, s_in, cw, cb, dtb, a_lane, dskip, g)


def _att_kernel(q_ref, k0_ref, k1_ref, k2_ref, k3_ref, v0_ref, v1_ref, v2_ref, v3_ref,
                t2_ref, g_ref, o_ref, s_ref, p_ref, linv_ref, acc_ref, *, rows):
    blk = pl.program_id(1)
    r0 = blk * ATT_ROWS
    k0row = jnp.clip(r0 - KH // 2, 0, rows - ATT_KROWS)
    lane = lax.broadcasted_iota(I32, (1, LANES), 1)
    left = lane < 64
    nq = ATT_ROWS * GRID_W
    n_pairs = ATT_KROWS // 2

    for hp in range(N_HEADS // 2):
        cols = slice(LANES * hp, LANES * (hp + 1))
        qp = q_ref[0, :, cols]
        kp = jnp.concatenate([r[0, :, cols] for r in (k0_ref, k1_ref, k2_ref, k3_ref)], axis=0)
        vp = jnp.concatenate([r[0, :, cols] for r in (v0_ref, v1_ref, v2_ref, v3_ref)], axis=0)
        o_pair = None
        for sub in range(2):
            h = 2 * hp + sub
            in_head = left if sub == 0 else jnp.logical_not(left)
            km = jnp.where(in_head, kp, jnp.zeros_like(kp))
            s_ref[...] = lax.dot_general(qp, km, (((1,), (1,)), ((), ())),
                                         preferred_element_type=F32)

            def row_body(i, carry, h=h):
                rq = pl.multiple_of(i * GRID_W, GRID_W)
                js = jnp.clip(r0 + i - KH // 2, 0, rows - KH) - k0row
                mx = jnp.full((GRID_W, LANES), NEG, F32)
                for jp in range(n_pairs):
                    jl = 2 * jp
                    vl = jnp.logical_and(jl >= js, jl < js + KH).astype(I32)
                    vr = jnp.logical_and(jl + 1 >= js, jl + 1 < js + KH).astype(I32)
                    midx = jnp.clip(k0row + jl - (r0 + i) + KH, 0, 2 * KH - 1)
                    valid = jnp.where(left, vl, vr) > 0
                    tcols = slice(LANES * jp, LANES * (jp + 1))
                    t = jnp.where(valid, s_ref[pl.ds(rq, GRID_W), tcols] + t2_ref[h, midx], NEG)
                    s_ref[pl.ds(rq, GRID_W), tcols] = t
                    mx = jnp.maximum(mx, t)
                m = jnp.max(mx, axis=1, keepdims=True)
                lsum = jnp.zeros((GRID_W, LANES), F32)
                for jp in range(n_pairs):
                    tcols = slice(LANES * jp, LANES * (jp + 1))
                    p = jnp.exp(s_ref[pl.ds(rq, GRID_W), tcols] - m)
                    lsum = lsum + p
                    p_ref[pl.ds(rq, GRID_W), tcols] = p.astype(BF16)
                tot = jnp.sum(lsum, axis=1, keepdims=True)
                linv_ref[pl.ds(rq, GRID_W), :] = jnp.broadcast_to(1.0 / tot, (GRID_W, LANES))
                return carry

            lax.fori_loop(0, ATT_ROWS, row_body, 0)
            o_full = jnp.dot(p_ref[...], vp, preferred_element_type=F32) * linv_ref[...]
            o_pair = o_full if sub == 0 else jnp.where(left, o_pair, o_full)
        acc_ref[:, cols] = o_pair
    del nq
    o_ref[0] = _rms(acc_ref[...], g_ref[...]).astype(BF16)


def _attention(q, k, v, t2, g):
    b, l, _ = q.shape
    rows = l // GRID_W
    assert rows >= ATT_KROWS and rows % ATT_ROWS == 0
    nblk = rows // ATT_ROWS
    nq = ATT_ROWS * GRID_W
    kb = 4 * GRID_W
    n_kb = l // kb

    def kspec(m):
        return pl.BlockSpec(
            (1, kb, D_ATT),
            lambda bi, i, m=m: (bi, jnp.clip(2 * i - 1, 0, n_kb - 4) + m, 0))

    const = lambda a: pl.BlockSpec(a.shape, lambda bi, i: (0,) * a.ndim)
    return pl.pallas_call(
        functools.partial(_att_kernel, rows=rows),
        grid=(b, nblk),
        in_specs=[pl.BlockSpec((1, nq, D_ATT), lambda bi, i: (bi, i, 0))]
                 + [kspec(m) for m in range(4)] + [kspec(m) for m in range(4)]
                 + [const(t2), const(g)],
        out_specs=pl.BlockSpec((1, nq, D_ATT), lambda bi, i: (bi, i, 0)),
        out_shape=jax.ShapeDtypeStruct((b, l, D_ATT), BF16),
        scratch_shapes=[pltpu.VMEM((nq, ATT_KROWS * GRID_W), F32),
                        pltpu.VMEM((nq, ATT_KROWS * GRID_W), BF16),
                        pltpu.VMEM((nq, LANES), F32),
                        pltpu.VMEM((nq, D_ATT), F32)],
        compiler_params=_cparams(("parallel", "parallel")),
    )(q, k, k, k, k, v, v, v, v, t2, g)


def _bias_tables(rpb):
    c = jnp.arange(GRID_W)
    col_start = jnp.clip(c - KW // 2, 0, GRID_W - KW)
    col_mask = (c[None, :] >= col_start[:, None]) & (c[None, :] < col_start[:, None] + KW)
    dc = jnp.clip(c[None, :] - c[:, None] + (KW - 1), 0, 2 * KW - 2)
    t = rpb.astype(F32)[:, :, dc]
    t = jnp.where(col_mask[None, None], t, NEG)
    pad = jnp.full((N_HEADS, 1, GRID_W, GRID_W), NEG, F32)
    t = jnp.concatenate([pad, t, pad], axis=1)
    return jnp.concatenate([t[:, :-1], t[:, 1:]], axis=-1)


def _out_proj_kernel(x_ref, ys_ref, ya_ref, wo_ref, g_ref, wr_ref, x1_ref, u_ref, aff_ref):
    x1 = x_ref[...] \
        + jnp.dot(ys_ref[...], wo_ref[:D_SSD, :], preferred_element_type=F32) \
        + jnp.dot(ya_ref[...], wo_ref[D_SSD:, :], preferred_element_type=F32)
    x1_ref[...] = x1
    ub = _rms(x1, g_ref[...]).astype(BF16)
    u_ref[...] = ub
    logits = jnp.dot(ub, wr_ref[...], preferred_element_type=F32)
    lt = logits.T[:N_EXPERTS, :]
    e = jnp.exp(lt - jnp.max(lt, axis=0, keepdims=True))
    aff_ref[...] = e / jnp.sum(e, axis=0, keepdims=True)


def _out_proj(x2d, ys, ya, wo, g, wr):
    t = x2d.shape[0]
    tm = min(ROW_TILE, t)
    row = lambda n: pl.BlockSpec((tm, n), lambda i: (i, 0))
    full = lambda a: pl.BlockSpec(a.shape, lambda i: (0, 0))
    return pl.pallas_call(
        _out_proj_kernel,
        grid=(t // tm,),
        in_specs=[row(D_MODEL), row(D_SSD), row(D_ATT), full(wo), full(g), full(wr)],
        out_specs=[row(D_MODEL), row(D_MODEL), pl.BlockSpec((N_EXPERTS, tm), lambda i: (0, i))],
        out_shape=[jax.ShapeDtypeStruct((t, D_MODEL), F32),
                   jax.ShapeDtypeStruct((t, D_MODEL), BF16),
                   jax.ShapeDtypeStruct((N_EXPERTS, t), F32)],
        compiler_params=_cparams(("parallel",)),
    )(x2d, ys, ya, wo, g, wr)


def _route_kernel(aff_ref, sel_ref, offs_ref, cnt_ref, *, cap, nb):
    aff = aff_ref[...]
    t = aff.shape[1]
    keys = lax.bitcast_convert_type(aff, I32)
    capf = jnp.float32(cap)

    def count(mask):
        return jnp.sum(jnp.where(mask, 1.0, 0.0), axis=1, keepdims=True)

    def key_step(i, cur):
        cand = cur | lax.shift_left(jnp.int32(1), 30 - i)
        return jnp.where(count(keys >= cand) >= capf, cand, cur)

    thr = lax.fori_loop(0, 31, key_step, jnp.zeros((N_EXPERTS, 1), I32))
    gt = keys > thr
    eq = keys == thr
    need = capf - count(gt)
    idx = lax.broadcasted_iota(I32, (N_EXPERTS, t), 1)
    nbits = max(t.bit_length(), 1)

    def idx_step(i, cur):
        cand = cur | lax.shift_left(jnp.int32(1), nbits - 1 - i)
        return jnp.where(count(jnp.logical_and(eq, idx < cand)) < need, cand, cur)

    last = lax.fori_loop(0, nbits, idx_step, jnp.zeros((N_EXPERTS, 1), I32))
    sel = jnp.where(gt, 1.0, jnp.where(jnp.logical_and(eq, idx <= last), 1.0, 0.0))
    sel_ref[...] = sel

    lane = lax.broadcasted_iota(I32, (N_EXPERTS, LANES), 1)
    cnt = jnp.zeros((N_EXPERTS, LANES), F32)
    for b in range(nb):
        cb = jnp.sum(sel[:, TOK_BLOCK * b:TOK_BLOCK * (b + 1)], axis=1, keepdims=True)
        cnt = jnp.where(lane == b, cb, cnt)
    incl = cnt
    sh = 1
    while sh < LANES:
        incl = incl + jnp.where(lane >= sh, pltpu.roll(incl, sh, axis=1), 0.0)
        sh *= 2
    offs_ref[...] = (incl - cnt).astype(I32)
    cnt_ref[...] = cnt.astype(I32)


def _route(aff_t):
    t = aff_t.shape[1]
    cap = CAP_FACTOR * t // N_EXPERTS
    nb = t // TOK_BLOCK
    assert nb <= LANES
    return pl.pallas_call(
        functools.partial(_route_kernel, cap=cap, nb=nb),
        out_shape=[jax.ShapeDtypeStruct((N_EXPERTS, t), F32),
                   jax.ShapeDtypeStruct((N_EXPERTS, LANES), I32),
                   jax.ShapeDtypeStruct((N_EXPERTS, LANES), I32)],
        compiler_params=pltpu.CompilerParams(vmem_limit_bytes=VMEM_LIMIT),
    )(aff_t)


def _slot_positions(sel):
    si = lax.broadcasted_iota(I32, (TOK_BLOCK, TOK_BLOCK), 0)
    ti = lax.broadcasted_iota(I32, (TOK_BLOCK, TOK_BLOCK), 1)
    tri = jnp.where(si <= ti, 1.0, 0.0).astype(BF16)
    incl = jnp.dot(sel.astype(BF16), tri, preferred_element_type=F32)
    return jnp.where(sel > 0.0, incl - 1.0, -1e6)


def _num_chunks(cnt_ref, b):
    mx = cnt_ref[0, b]
    for e in range(1, N_EXPERTS):
        mx = jnp.maximum(mx, cnt_ref[e, b])
    return (mx + SLOT_STEP - 1) // SLOT_STEP


def _align_down(v):
    return pl.multiple_of((v // SUBLANES) * SUBLANES, SUBLANES)


def _dispatch_kernel(offs_ref, cnt_ref, u_ref, sel_ref, aff_ref, xe_hbm,
                     g_ref, stage_ref, carry_ref, sem, n_ref, *, nb, cap):
    b = pl.program_id(0)

    def copy(slot, e, row):
        return pltpu.make_async_copy(stage_ref.at[slot, e],
                                     xe_hbm.at[e, pl.ds(row, SLOT_WIN), :], sem.at[slot])

    @pl.when(b == 0)
    def _():
        n_ref[0] = 0
        carry_ref[...] = jnp.zeros_like(carry_ref)
        stage_ref[1, 0] = jnp.zeros((SLOT_WIN, 640), U32)
        for e in range(N_EXPERTS):
            pltpu.make_async_copy(stage_ref.at[1, 0], xe_hbm.at[e, pl.ds(cap, SLOT_WIN), :],
                                  sem.at[1]).start()
        for e in range(N_EXPERTS):
            copy(1, 0, 0).wait()

    posm = _slot_positions(sel_ref[...])
    aff = aff_ref[...]
    ub = u_ref[...]
    w_iota = lax.broadcasted_iota(I32, (SLOT_WIN, TOK_BLOCK), 0).astype(F32)
    head_row = lax.broadcasted_iota(I32, (SUBLANES, 640), 0)

    def chunk(k, carry):
        n = n_ref[0]
        slot = lax.rem(n, 2)
        win = []
        for e in range(N_EXPERTS):
            cnt = cnt_ref[e, b]
            done = jnp.minimum(k * SLOT_STEP, cnt)
            base = offs_ref[e, b] + done
            row = _align_down(base)
            end = base + jnp.minimum((k + 1) * SLOT_STEP, cnt) - done
            win.append((row, base - row, _align_down(end) - row))
            ge = posm[e:e + 1, :] == w_iota + (done - (base - row)).astype(F32)
            g_ref[SLOT_WIN * e:SLOT_WIN * (e + 1), :] = jnp.where(ge, 1.0, 0.0).astype(BF16)
            gate = jnp.sum(jnp.where(ge, aff[e:e + 1, :], 0.0), axis=1, keepdims=True)
            stage_ref[slot, e, :, 512:] = _bits(jnp.broadcast_to(gate, (SLOT_WIN, LANES)))
        packed = _pack_halves(jnp.dot(g_ref[...], ub, preferred_element_type=F32))
        for e in range(N_EXPERTS):
            row, skew, nxt = win[e]
            stage_ref[slot, e, :, :512] = packed[SLOT_WIN * e:SLOT_WIN * (e + 1), :]
            stage_ref[slot, e, :SUBLANES, :] = jnp.where(
                head_row < skew, carry_ref[e], stage_ref[slot, e, :SUBLANES, :])
            carry_ref[e] = stage_ref[slot, e, pl.ds(pl.multiple_of(nxt, SUBLANES), SUBLANES), :]

        @pl.when(n > 0)
        def _():
            for e in range(N_EXPERTS):
                copy(1 - slot, e, 0).wait()

        for e in range(N_EXPERTS):
            copy(slot, e, win[e][0]).start()
        n_ref[0] = n + 1
        return carry

    lax.fori_loop(0, _num_chunks(cnt_ref, b), chunk, 0)

    @pl.when(jnp.logical_and(b == nb - 1, n_ref[0] > 0))
    def _():
        slot = lax.rem(n_ref[0] - 1, 2)
        for e in range(N_EXPERTS):
            copy(slot, e, 0).wait()


def _dispatch(u, sel, aff_t, offs, cnt):
    t = u.shape[0]
    cap = CAP_FACTOR * t // N_EXPERTS
    nb = t // TOK_BLOCK
    cap_p = cap + SLOT_WIN
    grid_spec = pltpu.PrefetchScalarGridSpec(
        num_scalar_prefetch=2,
        grid=(nb,),
        in_specs=[pl.BlockSpec((TOK_BLOCK, D_MODEL), lambda b, *_: (b, 0)),
                  pl.BlockSpec((N_EXPERTS, TOK_BLOCK), lambda b, *_: (0, b)),
                  pl.BlockSpec((N_EXPERTS, TOK_BLOCK), lambda b, *_: (0, b))],
        out_specs=pl.BlockSpec(memory_space=pl.ANY),
        scratch_shapes=[pltpu.VMEM((N_EXPERTS * SLOT_WIN, TOK_BLOCK), BF16),
                        pltpu.VMEM((2, N_EXPERTS, SLOT_WIN, 640), U32),
                        pltpu.VMEM((N_EXPERTS, SUBLANES, 640), U32),
                        pltpu.SemaphoreType.DMA((2,)),
                        pltpu.SMEM((1,), I32)])
    return pl.pallas_call(
        functools.partial(_dispatch_kernel, nb=nb, cap=cap),
        grid_spec=grid_spec,
        out_shape=jax.ShapeDtypeStruct((N_EXPERTS, cap_p, 640), U32),
        compiler_params=_cparams(("arbitrary",)),
    )(offs, cnt, u, sel, aff_t)


def _ffn_kernel(xe_ref, wg_ref, wu_ref, wd_ref, o_ref):
    xp = xe_ref[0]
    x = _unpack_halves(xp[:, :512])
    gate = lax.bitcast_convert_type(xp[:, 512:513], F32)
    gp = jnp.dot(x, wg_ref[0], preferred_element_type=F32)
    up = jnp.dot(x, wu_ref[0], preferred_element_type=F32)
    hid = (_silu(gp) * up).astype(BF16)
    out = jnp.dot(hid, wd_ref[0], preferred_element_type=F32) * gate
    o_ref[0] = _pack_halves(out.astype(BF16).astype(F32))


def _ffn(xe, wg, wu, wd, cap):
    tm = min(FFN_TILE, cap)
    return pl.pallas_call(
        _ffn_kernel,
        grid=(N_EXPERTS, cap // tm),
        in_specs=[pl.BlockSpec((1, tm, 640), lambda e, j: (e, j, 0)),
                  pl.BlockSpec((1, D_MODEL, D_FF), lambda e, j: (e, 0, 0)),
                  pl.BlockSpec((1, D_MODEL, D_FF), lambda e, j: (e, 0, 0)),
                  pl.BlockSpec((1, D_FF, D_MODEL), lambda e, j: (e, 0, 0))],
        out_specs=pl.BlockSpec((1, tm, 512), lambda e, j: (e, j, 0)),
        out_shape=jax.ShapeDtypeStruct((N_EXPERTS, cap, 512), U32),
        compiler_params=_cparams(("parallel", "parallel")),
    )(xe, wg, wu, wd)


def _combine_kernel(offs_ref, cnt_ref, x1_ref, sel_ref, gfin_ref, oe_hbm, o_ref,
                    g_ref, stage_ref, acc_ref, sem, *, nb, cap, final_norm):
    b = pl.program_id(0)
    w_iota = lax.broadcasted_iota(I32, (SLOT_WIN, TOK_BLOCK), 0).astype(F32)

    def window(bb, k, e):
        done = jnp.minimum(k * SLOT_STEP, cnt_ref[e, bb])
        base = offs_ref[e, bb] + done
        row = jnp.minimum(_align_down(base), cap - SLOT_WIN)
        return row, done - (base - row)

    def copy(slot, e, row):
        return pltpu.make_async_copy(oe_hbm.at[e, pl.ds(row, SLOT_WIN), :],
                                     stage_ref.at[slot, pl.ds(SLOT_WIN * e, SLOT_WIN), :],
                                     sem.at[slot])

    def fetch(bb, k, slot):
        for e in range(N_EXPERTS):
            copy(slot, e, pl.multiple_of(window(bb, k, e)[0], SUBLANES)).start()

    def wait(slot):
        for e in range(N_EXPERTS):
            copy(slot, e, 0).wait()

    @pl.when(b == 0)
    def _():
        fetch(0, 0, 0)

    @pl.when(b + 1 < nb)
    def _():
        fetch(b + 1, 0, lax.rem(b + 1, 2))

    pos_all = _slot_positions(sel_ref[...])

    def expand(slot, k):
        lo = (k * SLOT_STEP).astype(F32)
        posm = jnp.where(jnp.logical_and(pos_all >= lo, pos_all < lo + SLOT_STEP), pos_all, -1e6)
        for e in range(N_EXPERTS):
            wi = w_iota + window(b, k, e)[1].astype(F32)
            g_ref[SLOT_WIN * e:SLOT_WIN * (e + 1), :] = jnp.where(posm[e:e + 1, :] == wi, 1.0, 0.0)
        gt = g_ref[...].T.astype(BF16)
        slab = _unpack_halves(stage_ref[slot])
        return jnp.dot(gt, slab, preferred_element_type=F32)

    wait(lax.rem(b, 2))
    acc_ref[...] = x1_ref[...] + expand(lax.rem(b, 2), jnp.int32(0))

    def extra(k, carry):
        fetch(b, k, 2)
        wait(2)
        acc_ref[...] += expand(2, k)
        return carry

    lax.fori_loop(1, _num_chunks(cnt_ref, b), extra, 0)
    y = acc_ref[...]
    o_ref[...] = _rms(y, gfin_ref[...]) if final_norm else y


def _combine(x1, sel, oe, offs, cnt, gfin, final_norm):
    t = x1.shape[0]
    cap = CAP_FACTOR * t // N_EXPERTS
    nb = t // TOK_BLOCK
    grid_spec = pltpu.PrefetchScalarGridSpec(
        num_scalar_prefetch=2,
        grid=(nb,),
        in_specs=[pl.BlockSpec((TOK_BLOCK, D_MODEL), lambda b, *_: (b, 0)),
                  pl.BlockSpec((N_EXPERTS, TOK_BLOCK), lambda b, *_: (0, b)),
                  pl.BlockSpec((1, D_MODEL), lambda b, *_: (0, 0)),
                  pl.BlockSpec(memory_space=pl.ANY)],
        out_specs=pl.BlockSpec((TOK_BLOCK, D_MODEL), lambda b, *_: (b, 0)),
        scratch_shapes=[pltpu.VMEM((N_EXPERTS * SLOT_WIN, TOK_BLOCK), F32),
                        pltpu.VMEM((3, N_EXPERTS * SLOT_WIN, 512), U32),
                        pltpu.VMEM((TOK_BLOCK, D_MODEL), F32),
                        pltpu.SemaphoreType.DMA((3,))])
    return pl.pallas_call(
        functools.partial(_combine_kernel, nb=nb, cap=cap, final_norm=final_norm),
        grid_spec=grid_spec,
        out_shape=jax.ShapeDtypeStruct((t, D_MODEL), F32),
        compiler_params=_cparams(("arbitrary",)),
    )(offs, cnt, x1, sel, gfin, oe)


def _prep_layer(w_in, conv_w, conv_b, dt_bias, a_log, d_skip, rpb, w_out, w_router,
                w_gate, w_up, w_down):
    o1 = D_SSD
    o2 = o1 + CONV_DIM
    o3 = o2 + 2 * N_HEADS
    w = jnp.concatenate(
        [w_in[:, :o2], w_in[:, o3:], w_in[:, o2:o3],
         jnp.zeros((D_MODEL, LANES - 2 * N_HEADS), w_in.dtype)], axis=1).astype(BF16)
    lane_pad = lambda v: jnp.concatenate([v.reshape(-1).astype(F32),
                                          jnp.zeros((LANES - 2 * N_HEADS,), F32)])[None, :]
    return dict(
        w_in=w,
        conv_w=jnp.concatenate([conv_w.astype(F32), jnp.zeros((8 - CONV_W, CONV_DIM), F32)], 0),
        conv_b=conv_b.astype(F32)[None, :],
        dt_bias=lane_pad(dt_bias),
        a_lane=lane_pad(-jnp.exp(a_log.astype(F32))),
        d_skip=jnp.repeat(d_skip.astype(F32), 64)[None, :],
        t2=_bias_tables(rpb),
        w_out=w_out.astype(BF16),
        w_router=jnp.concatenate(
            [w_router, jnp.zeros((D_MODEL, LANES - N_EXPERTS), w_router.dtype)], 1).astype(BF16),
        w_gate=w_gate.astype(BF16), w_up=w_up.astype(BF16), w_down=w_down.astype(BF16))


def _layer(x2d, b, l, p, g_mix, g_ssd, g_att, g_ffn, g_final, final_norm):
    t = b * l
    row = lambda v: v.astype(F32)[None, :]
    z, xbc, q, k, v, dt = _in_proj(x2d, row(g_mix), p["w_in"])
    r3 = lambda a: a.reshape(b, l, a.shape[-1])
    y_ssd = _ssd(r3(xbc), r3(dt), r3(z), p["conv_w"], p["conv_b"], p["dt_bias"], p["a_lane"],
                 p["d_skip"], row(g_ssd))
    y_att = _attention(r3(q), r3(k), r3(v), p["t2"], row(g_att))
    x1, u, aff_t = _out_proj(x2d, y_ssd.reshape(t, D_SSD), y_att.reshape(t, D_ATT),
                             p["w_out"], row(g_ffn), p["w_router"])
    sel, offs, cnt = _route(aff_t)
    xe = _dispatch(u, sel, aff_t, offs, cnt)
    oe = _ffn(xe, p["w_gate"], p["w_up"], p["w_down"], CAP_FACTOR * t // N_EXPERTS)
    return _combine(x1, sel, oe, offs, cnt, row(g_final), final_norm)


def _trunk(x, layers, norm_mix_g, ssd_norm_g, attn_norm_g, norm_ffn_g, norm_final_g):
    b, l, _ = x.shape
    x2d = x.reshape(b * l, D_MODEL)
    depth = len(layers)
    for i, p in enumerate(layers):
        x2d = _layer(x2d, b, l, p, norm_mix_g[i], ssd_norm_g[i], attn_norm_g[i], norm_ffn_g[i],
                     norm_final_g, i == depth - 1)
    return x2d.reshape(b, l, D_MODEL)


def kernel(x_prompt, x_sample, norm_mix_g, w_in, conv_w, conv_b, dt_bias, a_log, d_skip, ssd_norm_g, attn_norm_g, rpb, w_out, norm_ffn_g, w_router, w_gate, w_up, w_down, norm_final_g):
    depth = w_in.shape[0]
    layers = [_prep_layer(w_in[i], conv_w[i], conv_b[i], dt_bias[i], a_log[i], d_skip[i], rpb[i],
                          w_out[i], w_router[i], w_gate[i], w_up[i], w_down[i])
              for i in range(depth)]
    run = functools.partial(_trunk, layers=layers, norm_mix_g=norm_mix_g, ssd_norm_g=ssd_norm_g,
                            attn_norm_g=attn_norm_g, norm_ffn_g=norm_ffn_g,
                            norm_final_g=norm_final_g)
    return (run(x_prompt), run(x_sample))
```

```python
import functools

import jax
import jax.numpy as jnp
from jax import lax
from jax.experimental import pallas as pl
from jax.experimental.pallas import tpu as pltpu

F32 = jnp.float32
BF16 = jnp.bfloat16
U32 = jnp.uint32
I32 = jnp.int32

D_MODEL = 1024
D_SSD = 512
D_ATT = 512
N_HEADS = 8
CONV_W = 5
CONV_DIM = 768
CHUNK = 128
GRID_W = 64
KH = 8
KW = 16
N_EXPERTS = 16
CAP_FACTOR = 2
D_FF = 2048
RMS_EPS = 1e-6
NEG = -1e30

LANES = 128
HALO = 16
IN_COLS_PADDED = 2944

ROW_TILE = 512
ATT_ROWS = 8
ATT_KROWS = 16
ATT_VISIT = 5
TOK_BLOCK = 256
SLOT_WIN = 64
SLOT_STEP = 56
SUBLANES = 8
FFN_TILE = 512
VMEM_LIMIT = 56 * 1024 * 1024


def _cparams(sem):
    return pltpu.CompilerParams(dimension_semantics=sem, vmem_limit_bytes=VMEM_LIMIT)


def _sigmoid(x):
    return 1.0 / (1.0 + jnp.exp(-x))


def _silu(x):
    return x * _sigmoid(x)


def _softplus(x):
    return jnp.maximum(x, 0.0) + jnp.log(1.0 + jnp.exp(-jnp.abs(x)))


def _rms(x, g):
    return x * lax.rsqrt(jnp.mean(x * x, axis=-1, keepdims=True) + RMS_EPS) * g


def _bits(x):
    return lax.bitcast_convert_type(x, U32)


def _pack_halves(x):
    lo = _bits(x[:, :512]) >> 16
    hi = _bits(x[:, 512:]) & jnp.uint32(0xFFFF0000)
    return hi | lo


def _unpack_halves(p):
    lo = lax.bitcast_convert_type(p << 16, F32)
    hi = lax.bitcast_convert_type(p & jnp.uint32(0xFFFF0000), F32)
    return jnp.concatenate([lo, hi], axis=1).astype(BF16)


def _in_proj_kernel(x_ref, g_ref, w_ref, z_ref, xbc_ref, q_ref, k_ref, v_ref, dt_ref):
    u = _rms(x_ref[...], g_ref[...]).astype(BF16)

    def proj(a, b):
        return jnp.dot(u, w_ref[:, a:b], preferred_element_type=F32)

    z_ref[...] = proj(0, 512).astype(BF16)
    xbc_ref[...] = proj(512, 1280).astype(BF16)
    q_ref[...] = (proj(1280, 1792) * (64 ** -0.5)).astype(BF16)
    k_ref[...] = proj(1792, 2304).astype(BF16)
    v_ref[...] = proj(2304, 2816).astype(BF16)
    dt_ref[...] = proj(2816, 2944)


def _in_proj(x2d, g, w):
    t = x2d.shape[0]
    tm = min(ROW_TILE, t)
    row = lambda n: pl.BlockSpec((tm, n), lambda i: (i, 0))
    full = lambda a: pl.BlockSpec(a.shape, lambda i: (0, 0))
    return pl.pallas_call(
        _in_proj_kernel,
        grid=(t // tm,),
        in_specs=[row(D_MODEL), full(g), full(w)],
        out_specs=[row(512), row(CONV_DIM), row(512), row(512), row(512), row(LANES)],
        out_shape=[jax.ShapeDtypeStruct((t, 512), BF16),
                   jax.ShapeDtypeStruct((t, CONV_DIM), BF16),
                   jax.ShapeDtypeStruct((t, 512), BF16),
                   jax.ShapeDtypeStruct((t, 512), BF16),
                   jax.ShapeDtypeStruct((t, 512), BF16),
                   jax.ShapeDtypeStruct((t, LANES), F32)],
        compiler_params=_cparams(("parallel",)),
    )(x2d, g, w)


def _conv_silu(xc_ref, xp_ref, xn_ref, cw_ref, cb_ref, c, nc):
    cur = xc_ref[0].astype(F32)
    prev = jnp.where(c > 0, xp_ref[0].astype(F32), 0.0)
    nxt = jnp.where(c < nc - 1, xn_ref[0].astype(F32), 0.0)
    xpad = jnp.concatenate([prev, cur, nxt], axis=0)
    acc = jnp.broadcast_to(cb_ref[...], (CHUNK, CONV_DIM))
    for k in range(CONV_W):
        start = HALO - CONV_W // 2 + k
        acc = acc + cw_ref[k:k + 1, :] * xpad[start:start + CHUNK, :]
    return _silu(acc)


def _decay_terms(dt_ref, dtb_ref, a_ref):
    dtv = _softplus(dt_ref[0] + dtb_ref[...])
    a = dtv * a_ref[...]
    li = lax.broadcasted_iota(I32, (CHUNK, CHUNK), 0)
    si = lax.broadcasted_iota(I32, (CHUNK, CHUNK), 1)
    tri = jnp.where(si <= li, 1.0, 0.0).astype(F32)
    cs = jnp.dot(tri, a, preferred_element_type=F32, precision=lax.Precision.HIGHEST)
    tot = cs[CHUNK - 1:CHUNK, :]
    return dtv, cs, cs - a, tot


def _pair_cols(x, h0, left):
    return jnp.where(left, x[:, h0:h0 + 1], x[:, h0 + 1:h0 + 2])


def _ssd_bwd_state_kernel(xc_ref, xp_ref, xn_ref, dt_ref, cw_ref, cb_ref, dtb_ref, a_ref,
                          sin_ref, sb_ref, *, nc):
    i = pl.program_id(1)
    c = nc - 1 - i

    @pl.when(i == 0)
    def _():
        sb_ref[...] = jnp.zeros_like(sb_ref)

    sin_ref[0, 0] = sb_ref[...]
    act = _conv_silu(xc_ref, xp_ref, xn_ref, cw_ref, cb_ref, c, nc)
    xs = act[:, :D_SSD]
    bt = act[:, D_SSD:D_SSD + LANES].T.astype(BF16)
    dtv, _, ex, tot = _decay_terms(dt_ref, dtb_ref, a_ref)
    wb = jnp.exp(ex) * dtv
    decb = jnp.exp(tot)
    left = lax.broadcasted_iota(I32, (1, LANES), 1) < 64
    for j in range(4):
        xw = (xs[:, LANES * j:LANES * (j + 1)] * _pair_cols(wb, 8 + 2 * j, left)).astype(BF16)
        st = jnp.dot(bt, xw, preferred_element_type=F32)
        sb_ref[j] = _pair_cols(decb, 8 + 2 * j, left) * sb_ref[j] + st


def _ssd_main_kernel(xc_ref, xp_ref, xn_ref, dt_ref, z_ref, sin_ref, cw_ref, cb_ref, dtb_ref,
                     a_ref, dskip_ref, g_ref, o_ref, sf_ref, y_ref, *, nc):
    c = pl.program_id(1)

    @pl.when(c == 0)
    def _():
        sf_ref[...] = jnp.zeros_like(sf_ref)

    act = _conv_silu(xc_ref, xp_ref, xn_ref, cw_ref, cb_ref, c, nc)
    xs = act[:, :D_SSD]
    bm = act[:, D_SSD:D_SSD + LANES]
    cm = act[:, D_SSD + LANES:]
    dtv, cs, ex, tot = _decay_terms(dt_ref, dtb_ref, a_ref)
    cs_t, ex_t, dt_t = cs.T, ex.T, dtv.T
    wf = jnp.exp(tot - cs) * dtv
    ef = jnp.exp(cs)
    eb = jnp.exp(tot - ex)
    decf = jnp.exp(tot)

    lane = lax.broadcasted_iota(I32, (1, LANES), 1)
    left = lane < 64
    li = lax.broadcasted_iota(I32, (CHUNK, CHUNK), 0)
    si = lax.broadcasted_iota(I32, (CHUNK, CHUNK), 1)
    bm_b = bm.astype(BF16)
    cm_b = cm.astype(BF16)
    bt = bm.T.astype(BF16)
    zero = jnp.zeros_like(bm_b)
    for grp in range(2):
        in_grp = left if grp == 0 else jnp.logical_not(left)
        b_g = jnp.where(in_grp, bm_b, zero)
        c_g = jnp.where(in_grp, cm_b, zero)
        cb_g = lax.dot_general(cm_b, b_g, (((1,), (1,)), ((), ())),
                               preferred_element_type=F32)
        for jj in range(2):
            j = 2 * grp + jj
            x_p = xs[:, LANES * j:LANES * (j + 1)]
            x_pb = x_p.astype(BF16)
            halves = []
            for sub in range(2):
                h = 2 * j + sub
                dtf = dt_t[h:h + 1, :]
                dtb = dt_t[8 + h:9 + h, :]
                arg = jnp.where(si <= li,
                                cs[:, h:h + 1] - cs_t[h:h + 1, :],
                                ex_t[8 + h:9 + h, :] - ex[:, 8 + h:9 + h])
                dsel = jnp.where(si < li, dtf, jnp.where(si > li, dtb, dtf + dtb))
                m = (cb_g * jnp.exp(arg) * dsel).astype(BF16)
                halves.append(jnp.dot(m, x_pb, preferred_element_type=F32))
            y = jnp.where(left, halves[0], halves[1])
            sf = sf_ref[j]
            sb = sin_ref[0, 0, j]
            y = y + jnp.dot(c_g, sf.astype(BF16), preferred_element_type=F32) \
                * _pair_cols(ef, 2 * j, left)
            y = y + jnp.dot(c_g, sb.astype(BF16), preferred_element_type=F32) \
                * _pair_cols(eb, 8 + 2 * j, left)
            y_ref[:, LANES * j:LANES * (j + 1)] = y + dskip_ref[:, LANES * j:LANES * (j + 1)] * x_p
            xw = (x_p * _pair_cols(wf, 2 * j, left)).astype(BF16)
            st = jnp.dot(bt, xw, preferred_element_type=F32)
            sf_ref[j] = _pair_cols(decf, 2 * j, left) * sf + st

    yz = y_ref[...] * _silu(z_ref[0].astype(F32))
    o_ref[0] = _rms(yz, g_ref[...]).astype(BF16)


def _ssd(xbc, dt, z, cw, cb, dtb, a_lane, dskip, g):
    b, l, _ = xbc.shape
    nc = l // CHUNK
    hb = CHUNK // HALO
    last_halo = l // HALO - 1

    def specs(chunk_of):
        cur = pl.BlockSpec((1, CHUNK, CONV_DIM), lambda bi, i: (bi, chunk_of(i), 0))
        prev = pl.BlockSpec((1, HALO, CONV_DIM),
                            lambda bi, i: (bi, jnp.maximum(chunk_of(i) * hb - 1, 0), 0))
        nxt = pl.BlockSpec((1, HALO, CONV_DIM),
                           lambda bi, i: (bi, jnp.minimum((chunk_of(i) + 1) * hb, last_halo), 0))
        dts = pl.BlockSpec((1, CHUNK, LANES), lambda bi, i: (bi, chunk_of(i), 0))
        return cur, prev, nxt, dts

    const = lambda a: pl.BlockSpec(a.shape, lambda bi, i: (0,) * a.ndim)

    rev = lambda i: nc - 1 - i
    cur, prev, nxt, dts = specs(rev)
    s_in = pl.pallas_call(
        functools.partial(_ssd_bwd_state_kernel, nc=nc),
        grid=(b, nc),
        in_specs=[cur, prev, nxt, dts, const(cw), const(cb), const(dtb), const(a_lane)],
        out_specs=pl.BlockSpec((1, 1, 4, LANES, LANES), lambda bi, i: (bi, rev(i), 0, 0, 0)),
        out_shape=jax.ShapeDtypeStruct((b, nc, 4, LANES, LANES), F32),
        scratch_shapes=[pltpu.VMEM((4, LANES, LANES), F32)],
        compiler_params=_cparams(("parallel", "arbitrary")),
    )(xbc, xbc, xbc, dt, cw, cb, dtb, a_lane)

    fwd = lambda i: i
    cur, prev, nxt, dts = specs(fwd)
    return pl.pallas_call(
        functools.partial(_ssd_main_kernel, nc=nc),
        grid=(b, nc),
        in_specs=[cur, prev, nxt, dts,
                  pl.BlockSpec((1, CHUNK, D_SSD), lambda bi, i: (bi, i, 0)),
                  pl.BlockSpec((1, 1, 4, LANES, LANES), lambda bi, i: (bi, i, 0, 0, 0)),
                  const(cw), const(cb), const(dtb), const(a_lane), const(dskip), const(g)],
        out_specs=pl.BlockSpec((1, CHUNK, D_SSD), lambda bi, i: (bi, i, 0)),
        out_shape=jax.ShapeDtypeStruct((b, l, D_SSD), BF16),
        scratch_shapes=[pltpu.VMEM((4, LANES, LANES), F32), pltpu.VMEM((CHUNK, D_SSD), F32)],
        compiler_params=_cparams(("parallel", "arbitrary")),
    )(xbc, xbc, xbc, dt, z, s_in, cw, cb, dtb, a_lane, dskip, g)


def _att_kernel(q_ref, k0_ref, k1_ref, k2_ref, k3_ref, v0_ref, v1_ref, v2_ref, v3_ref,
                t2_ref, g_ref, o_ref, s_ref, t_ref, p_ref, linv_ref, acc_ref, *, rows):
    blk = pl.program_id(1)
    lane = lax.broadcasted_iota(I32, (1, LANES), 1)
    left = lane < 64
    n_pairs = ATT_KROWS // 2
    zero_tile = jnp.zeros((GRID_W, LANES), BF16)

    def softmax_tail(par, rq, mx, kcols_of, n_tiles):
        m = jnp.max(mx, axis=1, keepdims=True)
        lsum = jnp.zeros((GRID_W, LANES), F32)
        for d in range(n_tiles):
            p = jnp.exp(t_ref[par, rq, LANES * d:LANES * (d + 1)] - m)
            lsum = lsum + p
            p_ref[par, rq, kcols_of(d)] = p.astype(BF16)
        tot = jnp.sum(lsum, axis=1, keepdims=True)
        linv_ref[par, rq, :] = jnp.broadcast_to(1.0 / tot, (GRID_W, LANES))

    def softmax_row(i, h, par, place):
        koff = {"top": 0, "interior": -(KH // 2), "bottom": -KH}[place]
        js = {"top": max(i - KH // 2, 0), "interior": i, "bottom": min(i + KH // 2, KH)}[place]
        rq = slice(GRID_W * i, GRID_W * (i + 1))
        first, last = js // 2, (js + KH - 1) // 2
        kcols_of = lambda d: slice(LANES * (first + d), LANES * (first + d + 1))
        mx = None
        for d in range(last - first + 1):
            jl = 2 * (first + d)
            t = s_ref[par, rq, kcols_of(d)] + t2_ref[h, koff + jl - i + KH]
            if jl < js:
                t = jnp.where(left, NEG, t)
            if jl + 1 > js + KH - 1:
                t = jnp.where(left, t, NEG)
            t_ref[par, rq, LANES * d:LANES * (d + 1)] = t
            mx = t if mx is None else jnp.maximum(mx, t)
        softmax_tail(par, rq, mx, kcols_of, last - first + 1)
        for jp in range(n_pairs):
            if jp < first or jp > last:
                p_ref[par, rq, LANES * jp:LANES * (jp + 1)] = zero_tile

    def all_heads(place):
        for hp in range(N_HEADS // 2):
            cols = slice(LANES * hp, LANES * (hp + 1))
            qp = q_ref[0, :, cols]
            kp = jnp.concatenate([r[0, :, cols] for r in (k0_ref, k1_ref, k2_ref, k3_ref)], axis=0)
            vp = jnp.concatenate([r[0, :, cols] for r in (v0_ref, v1_ref, v2_ref, v3_ref)], axis=0)
            o_pair = None
            for sub in range(2):
                h = 2 * hp + sub
                in_head = left if sub == 0 else jnp.logical_not(left)
                km = jnp.where(in_head, kp, jnp.zeros_like(kp))
                s_ref[sub] = lax.dot_general(qp, km, (((1,), (1,)), ((), ())),
                                             preferred_element_type=F32)
                for i in range(ATT_ROWS):
                    softmax_row(i, h, sub, place)
                o_full = jnp.dot(p_ref[sub], vp, preferred_element_type=F32) * linv_ref[sub]
                o_pair = o_full if sub == 0 else jnp.where(left, o_pair, o_full)
            acc_ref[:, cols] = o_pair

    last_blk = rows // ATT_ROWS - 1
    pl.when(blk == 0)(lambda: all_heads("top"))
    pl.when(jnp.logical_and(blk > 0, blk < last_blk))(lambda: all_heads("interior"))
    pl.when(blk == last_blk)(lambda: all_heads("bottom"))
    o_ref[0] = _rms(acc_ref[...], g_ref[...]).astype(BF16)


def _attention(q, k, v, t2, g):
    b, l, _ = q.shape
    rows = l // GRID_W
    assert rows >= ATT_KROWS and rows % ATT_ROWS == 0
    nblk = rows // ATT_ROWS
    nq = ATT_ROWS * GRID_W
    kb = 4 * GRID_W
    n_kb = l // kb

    def kspec(m):
        return pl.BlockSpec(
            (1, kb, D_ATT),
            lambda bi, i, m=m: (bi, jnp.clip(2 * i - 1, 0, n_kb - 4) + m, 0))

    const = lambda a: pl.BlockSpec(a.shape, lambda bi, i: (0,) * a.ndim)
    return pl.pallas_call(
        functools.partial(_att_kernel, rows=rows),
        grid=(b, nblk),
        in_specs=[pl.BlockSpec((1, nq, D_ATT), lambda bi, i: (bi, i, 0))]
                 + [kspec(m) for m in range(4)] + [kspec(m) for m in range(4)]
                 + [const(t2), const(g)],
        out_specs=pl.BlockSpec((1, nq, D_ATT), lambda bi, i: (bi, i, 0)),
        out_shape=jax.ShapeDtypeStruct((b, l, D_ATT), BF16),
        scratch_shapes=[pltpu.VMEM((2, nq, ATT_KROWS * GRID_W), F32),
                        pltpu.VMEM((2, nq, ATT_VISIT * LANES), F32),
                        pltpu.VMEM((2, nq, ATT_KROWS * GRID_W), BF16),
                        pltpu.VMEM((2, nq, LANES), F32),
                        pltpu.VMEM((nq, D_ATT), F32)],
        compiler_params=_cparams(("parallel", "parallel")),
    )(q, k, k, k, k, v, v, v, v, t2, g)


def _bias_tables(rpb):
    c = jnp.arange(GRID_W)
    col_start = jnp.clip(c - KW // 2, 0, GRID_W - KW)
    col_mask = (c[None, :] >= col_start[:, None]) & (c[None, :] < col_start[:, None] + KW)
    dc = jnp.clip(c[None, :] - c[:, None] + (KW - 1), 0, 2 * KW - 2)
    t = rpb.astype(F32)[:, :, dc]
    t = jnp.where(col_mask[None, None], t, NEG)
    pad = jnp.full((N_HEADS, 1, GRID_W, GRID_W), NEG, F32)
    t = jnp.concatenate([pad, t, pad], axis=1)
    return jnp.concatenate([t[:, :-1], t[:, 1:]], axis=-1)


def _out_proj_kernel(x_ref, ys_ref, ya_ref, wo_ref, g_ref, wr_ref, x1_ref, u_ref, aff_ref):
    x1 = x_ref[...] \
        + jnp.dot(ys_ref[...], wo_ref[:D_SSD, :], preferred_element_type=F32) \
        + jnp.dot(ya_ref[...], wo_ref[D_SSD:, :], preferred_element_type=F32)
    x1_ref[...] = x1
    ub = _rms(x1, g_ref[...]).astype(BF16)
    u_ref[...] = ub
    logits = jnp.dot(ub, wr_ref[...], preferred_element_type=F32)
    lt = logits.T[:N_EXPERTS, :]
    e = jnp.exp(lt - jnp.max(lt, axis=0, keepdims=True))
    aff_ref[...] = e / jnp.sum(e, axis=0, keepdims=True)


def _out_proj(x2d, ys, ya, wo, g, wr):
    t = x2d.shape[0]
    tm = min(ROW_TILE, t)
    row = lambda n: pl.BlockSpec((tm, n), lambda i: (i, 0))
    full = lambda a: pl.BlockSpec(a.shape, lambda i: (0, 0))
    return pl.pallas_call(
        _out_proj_kernel,
        grid=(t // tm,),
        in_specs=[row(D_MODEL), row(D_SSD), row(D_ATT), full(wo), full(g), full(wr)],
        out_specs=[row(D_MODEL), row(D_MODEL), pl.BlockSpec((N_EXPERTS, tm), lambda i: (0, i))],
        out_shape=[jax.ShapeDtypeStruct((t, D_MODEL), F32),
                   jax.ShapeDtypeStruct((t, D_MODEL), BF16),
                   jax.ShapeDtypeStruct((N_EXPERTS, t), F32)],
        compiler_params=_cparams(("parallel",)),
    )(x2d, ys, ya, wo, g, wr)


def _route_kernel(aff_ref, sel_ref, offs_ref, cnt_ref, *, cap, nb):
    aff = aff_ref[...]
    t = aff.shape[1]
    keys = lax.bitcast_convert_type(aff, I32)
    capf = jnp.float32(cap)

    def count(mask):
        return jnp.sum(jnp.where(mask, 1.0, 0.0), axis=1, keepdims=True)

    def key_step(i, cur):
        cand = cur | lax.shift_left(jnp.int32(1), 30 - i)
        return jnp.where(count(keys >= cand) >= capf, cand, cur)

    thr = lax.fori_loop(0, 31, key_step, jnp.zeros((N_EXPERTS, 1), I32))
    gt = keys > thr
    eq = keys == thr
    need = capf - count(gt)
    idx = lax.broadcasted_iota(I32, (N_EXPERTS, t), 1)
    nbits = max(t.bit_length(), 1)

    def idx_step(i, cur):
        cand = cur | lax.shift_left(jnp.int32(1), nbits - 1 - i)
        return jnp.where(count(jnp.logical_and(eq, idx < cand)) < need, cand, cur)

    last = lax.fori_loop(0, nbits, idx_step, jnp.zeros((N_EXPERTS, 1), I32))
    sel = jnp.where(gt, 1.0, jnp.where(jnp.logical_and(eq, idx <= last), 1.0, 0.0))
    sel_ref[...] = sel

    lane = lax.broadcasted_iota(I32, (N_EXPERTS, LANES), 1)
    cnt = jnp.zeros((N_EXPERTS, LANES), F32)
    for b in range(nb):
        cb = jnp.sum(sel[:, TOK_BLOCK * b:TOK_BLOCK * (b + 1)], axis=1, keepdims=True)
        cnt = jnp.where(lane == b, cb, cnt)
    incl = cnt
    sh = 1
    while sh < LANES:
        incl = incl + jnp.where(lane >= sh, pltpu.roll(incl, sh, axis=1), 0.0)
        sh *= 2
    offs_ref[...] = (incl - cnt).astype(I32)
    cnt_ref[...] = cnt.astype(I32)


def _route(aff_t):
    t = aff_t.shape[1]
    cap = CAP_FACTOR * t // N_EXPERTS
    nb = t // TOK_BLOCK
    assert nb <= LANES
    return pl.pallas_call(
        functools.partial(_route_kernel, cap=cap, nb=nb),
        out_shape=[jax.ShapeDtypeStruct((N_EXPERTS, t), F32),
                   jax.ShapeDtypeStruct((N_EXPERTS, LANES), I32),
                   jax.ShapeDtypeStruct((N_EXPERTS, LANES), I32)],
        compiler_params=pltpu.CompilerParams(vmem_limit_bytes=VMEM_LIMIT),
    )(aff_t)


def _slot_positions(sel):
    si = lax.broadcasted_iota(I32, (TOK_BLOCK, TOK_BLOCK), 0)
    ti = lax.broadcasted_iota(I32, (TOK_BLOCK, TOK_BLOCK), 1)
    tri = jnp.where(si <= ti, 1.0, 0.0).astype(BF16)
    incl = jnp.dot(sel.astype(BF16), tri, preferred_element_type=F32)
    return jnp.where(sel > 0.0, incl - 1.0, -1e6)


def _num_chunks(cnt_ref, b):
    mx = cnt_ref[0, b]
    for e in range(1, N_EXPERTS):
        mx = jnp.maximum(mx, cnt_ref[e, b])
    return (mx + SLOT_STEP - 1) // SLOT_STEP


def _align_down(v):
    return pl.multiple_of((v // SUBLANES) * SUBLANES, SUBLANES)


def _dispatch_kernel(offs_ref, cnt_ref, u_ref, sel_ref, aff_ref, xe_hbm,
                     g_ref, stage_ref, carry_ref, sem, n_ref, *, nb, cap):
    b = pl.program_id(0)

    def copy(slot, e, row):
        return pltpu.make_async_copy(stage_ref.at[slot, e],
                                     xe_hbm.at[e, pl.ds(row, SLOT_WIN), :], sem.at[slot])

    @pl.when(b == 0)
    def _():
        n_ref[0] = 0
        carry_ref[...] = jnp.zeros_like(carry_ref)
        stage_ref[1, 0] = jnp.zeros((SLOT_WIN, 640), U32)
        for e in range(N_EXPERTS):
            pltpu.make_async_copy(stage_ref.at[1, 0], xe_hbm.at[e, pl.ds(cap, SLOT_WIN), :],
                                  sem.at[1]).start()
        for e in range(N_EXPERTS):
            copy(1, 0, 0).wait()

    posm = _slot_positions(sel_ref[...])
    aff = aff_ref[...]
    ub = u_ref[...]
    w_iota = lax.broadcasted_iota(I32, (SLOT_WIN, TOK_BLOCK), 0).astype(F32)
    head_row = lax.broadcasted_iota(I32, (SUBLANES, 640), 0)

    def chunk(k, carry):
        n = n_ref[0]
        slot = lax.rem(n, 2)
        win = []
        for e in range(N_EXPERTS):
            cnt = cnt_ref[e, b]
            done = jnp.minimum(k * SLOT_STEP, cnt)
            base = offs_ref[e, b] + done
            row = _align_down(base)
            end = base + jnp.minimum((k + 1) * SLOT_STEP, cnt) - done
            win.append((row, base - row, _align_down(end) - row))
            ge = posm[e:e + 1, :] == w_iota + (done - (base - row)).astype(F32)
            g_ref[SLOT_WIN * e:SLOT_WIN * (e + 1), :] = jnp.where(ge, 1.0, 0.0).astype(BF16)
            gate = jnp.sum(jnp.where(ge, aff[e:e + 1, :], 0.0), axis=1, keepdims=True)
            stage_ref[slot, e, :, 512:] = _bits(jnp.broadcast_to(gate, (SLOT_WIN, LANES)))
        packed = _pack_halves(jnp.dot(g_ref[...], ub, preferred_element_type=F32))
        for e in range(N_EXPERTS):
            row, skew, nxt = win[e]
            stage_ref[slot, e, :, :512] = packed[SLOT_WIN * e:SLOT_WIN * (e + 1), :]
            stage_ref[slot, e, :SUBLANES, :] = jnp.where(
                head_row < skew, carry_ref[e], stage_ref[slot, e, :SUBLANES, :])
            carry_ref[e] = stage_ref[slot, e, pl.ds(pl.multiple_of(nxt, SUBLANES), SUBLANES), :]

        @pl.when(n > 0)
        def _():
            for e in range(N_EXPERTS):
                copy(1 - slot, e, 0).wait()

        for e in range(N_EXPERTS):
            copy(slot, e, win[e][0]).start()
        n_ref[0] = n + 1
        return carry

    lax.fori_loop(0, _num_chunks(cnt_ref, b), chunk, 0)

    @pl.when(jnp.logical_and(b == nb - 1, n_ref[0] > 0))
    def _():
        slot = lax.rem(n_ref[0] - 1, 2)
        for e in range(N_EXPERTS):
            copy(slot, e, 0).wait()


def _dispatch(u, sel, aff_t, offs, cnt):
    t = u.shape[0]
    cap = CAP_FACTOR * t // N_EXPERTS
    nb = t // TOK_BLOCK
    cap_p = cap + SLOT_WIN
    grid_spec = pltpu.PrefetchScalarGridSpec(
        num_scalar_prefetch=2,
        grid=(nb,),
        in_specs=[pl.BlockSpec((TOK_BLOCK, D_MODEL), lambda b, *_: (b, 0)),
                  pl.BlockSpec((N_EXPERTS, TOK_BLOCK), lambda b, *_: (0, b)),
                  pl.BlockSpec((N_EXPERTS, TOK_BLOCK), lambda b, *_: (0, b))],
        out_specs=pl.BlockSpec(memory_space=pl.ANY),
        scratch_shapes=[pltpu.VMEM((N_EXPERTS * SLOT_WIN, TOK_BLOCK), BF16),
                        pltpu.VMEM((2, N_EXPERTS, SLOT_WIN, 640), U32),
                        pltpu.VMEM((N_EXPERTS, SUBLANES, 640), U32),
                        pltpu.SemaphoreType.DMA((2,)),
                        pltpu.SMEM((1,), I32)])
    return pl.pallas_call(
        functools.partial(_dispatch_kernel, nb=nb, cap=cap),
        grid_spec=grid_spec,
        out_shape=jax.ShapeDtypeStruct((N_EXPERTS, cap_p, 640), U32),
        compiler_params=_cparams(("arbitrary",)),
    )(offs, cnt, u, sel, aff_t)


def _ffn_kernel(xe_ref, wg_ref, wu_ref, wd_ref, o_ref):
    xp = xe_ref[0]
    x = _unpack_halves(xp[:, :512])
    gate = lax.bitcast_convert_type(xp[:, 512:513], F32)
    gp = jnp.dot(x, wg_ref[0, 0], preferred_element_type=F32)
    up = jnp.dot(x, wu_ref[0, 0], preferred_element_type=F32)
    hid = (_silu(gp) * up).astype(BF16)
    out = jnp.dot(hid, wd_ref[0, 0], preferred_element_type=F32) * gate
    o_ref[0] = _pack_halves(out.astype(BF16).astype(F32))


def _ffn(xe, wg, wu, wd, layer, cap):
    tm = min(FFN_TILE, cap)
    return pl.pallas_call(
        _ffn_kernel,
        grid=(N_EXPERTS, cap // tm),
        in_specs=[pl.BlockSpec((1, tm, 640), lambda e, j: (e, j, 0)),
                  pl.BlockSpec((1, 1, D_MODEL, D_FF), lambda e, j: (layer, e, 0, 0)),
                  pl.BlockSpec((1, 1, D_MODEL, D_FF), lambda e, j: (layer, e, 0, 0)),
                  pl.BlockSpec((1, 1, D_FF, D_MODEL), lambda e, j: (layer, e, 0, 0))],
        out_specs=pl.BlockSpec((1, tm, 512), lambda e, j: (e, j, 0)),
        out_shape=jax.ShapeDtypeStruct((N_EXPERTS, cap, 512), U32),
        compiler_params=_cparams(("parallel", "parallel")),
    )(xe, wg, wu, wd)


def _combine_kernel(offs_ref, cnt_ref, x1_ref, sel_ref, gfin_ref, oe_hbm, o_ref,
                    g_ref, stage_ref, acc_ref, sem, *, nb, cap, final_norm):
    b = pl.program_id(0)
    w_iota = lax.broadcasted_iota(I32, (SLOT_WIN, TOK_BLOCK), 0).astype(F32)

    def window(bb, k, e):
        done = jnp.minimum(k * SLOT_STEP, cnt_ref[e, bb])
        base = offs_ref[e, bb] + done
        row = jnp.minimum(_align_down(base), cap - SLOT_WIN)
        return row, done - (base - row)

    def copy(slot, e, row):
        return pltpu.make_async_copy(oe_hbm.at[e, pl.ds(row, SLOT_WIN), :],
                                     stage_ref.at[slot, pl.ds(SLOT_WIN * e, SLOT_WIN), :],
                                     sem.at[slot])

    def fetch(bb, k, slot):
        for e in range(N_EXPERTS):
            copy(slot, e, pl.multiple_of(window(bb, k, e)[0], SUBLANES)).start()

    def wait(slot):
        for e in range(N_EXPERTS):
            copy(slot, e, 0).wait()

    @pl.when(b == 0)
    def _():
        fetch(0, 0, 0)

    @pl.when(b + 1 < nb)
    def _():
        fetch(b + 1, 0, lax.rem(b + 1, 2))

    pos_all = _slot_positions(sel_ref[...])

    def expand(slot, k):
        lo = (k * SLOT_STEP).astype(F32)
        posm = jnp.where(jnp.logical_and(pos_all >= lo, pos_all < lo + SLOT_STEP), pos_all, -1e6)
        for e in range(N_EXPERTS):
            wi = w_iota + window(b, k, e)[1].astype(F32)
            g_ref[SLOT_WIN * e:SLOT_WIN * (e + 1), :] = jnp.where(posm[e:e + 1, :] == wi, 1.0, 0.0)
        gt = g_ref[...].T.astype(BF16)
        slab = _unpack_halves(stage_ref[slot])
        return jnp.dot(gt, slab, preferred_element_type=F32)

    wait(lax.rem(b, 2))
    acc_ref[...] = x1_ref[...] + expand(lax.rem(b, 2), jnp.int32(0))

    def extra(k, carry):
        fetch(b, k, 2)
        wait(2)
        acc_ref[...] += expand(2, k)
        return carry

    lax.fori_loop(1, _num_chunks(cnt_ref, b), extra, 0)
    y = acc_ref[...]
    o_ref[...] = _rms(y, gfin_ref[...]) if final_norm else y


def _combine(x1, sel, oe, offs, cnt, gfin, final_norm):
    t = x1.shape[0]
    cap = CAP_FACTOR * t // N_EXPERTS
    nb = t // TOK_BLOCK
    grid_spec = pltpu.PrefetchScalarGridSpec(
        num_scalar_prefetch=2,
        grid=(nb,),
        in_specs=[pl.BlockSpec((TOK_BLOCK, D_MODEL), lambda b, *_: (b, 0)),
                  pl.BlockSpec((N_EXPERTS, TOK_BLOCK), lambda b, *_: (0, b)),
                  pl.BlockSpec((1, D_MODEL), lambda b, *_: (0, 0)),
                  pl.BlockSpec(memory_space=pl.ANY)],
        out_specs=pl.BlockSpec((TOK_BLOCK, D_MODEL), lambda b, *_: (b, 0)),
        scratch_shapes=[pltpu.VMEM((N_EXPERTS * SLOT_WIN, TOK_BLOCK), F32),
                        pltpu.VMEM((3, N_EXPERTS * SLOT_WIN, 512), U32),
                        pltpu.VMEM((TOK_BLOCK, D_MODEL), F32),
                        pltpu.SemaphoreType.DMA((3,))])
    return pl.pallas_call(
        functools.partial(_combine_kernel, nb=nb, cap=cap, final_norm=final_norm),
        grid_spec=grid_spec,
        out_shape=jax.ShapeDtypeStruct((t, D_MODEL), F32),
        compiler_params=_cparams(("arbitrary",)),
    )(offs, cnt, x1, sel, gfin, oe)


def _prep_layer(w_in, conv_w, conv_b, dt_bias, a_log, d_skip, rpb, w_out, w_router):
    o1 = D_SSD
    o2 = o1 + CONV_DIM
    o3 = o2 + 2 * N_HEADS
    w = jnp.concatenate(
        [w_in[:, :o2], w_in[:, o3:], w_in[:, o2:o3],
         jnp.zeros((D_MODEL, LANES - 2 * N_HEADS), w_in.dtype)], axis=1).astype(BF16)
    lane_pad = lambda v: jnp.concatenate([v.reshape(-1).astype(F32),
                                          jnp.zeros((LANES - 2 * N_HEADS,), F32)])[None, :]
    return dict(
        w_in=w,
        conv_w=jnp.concatenate([conv_w.astype(F32), jnp.zeros((8 - CONV_W, CONV_DIM), F32)], 0),
        conv_b=conv_b.astype(F32)[None, :],
        dt_bias=lane_pad(dt_bias),
        a_lane=lane_pad(-jnp.exp(a_log.astype(F32))),
        d_skip=jnp.repeat(d_skip.astype(F32), 64)[None, :],
        t2=_bias_tables(rpb),
        w_out=w_out.astype(BF16),
        w_router=jnp.concatenate(
            [w_router, jnp.zeros((D_MODEL, LANES - N_EXPERTS), w_router.dtype)], 1).astype(BF16))


def _layer(x2d, b, l, p, experts, layer, g_mix, g_ssd, g_att, g_ffn, g_final, final_norm):
    t = b * l
    row = lambda v: v.astype(F32)[None, :]
    z, xbc, q, k, v, dt = _in_proj(x2d, row(g_mix), p["w_in"])
    r3 = lambda a: a.reshape(b, l, a.shape[-1])
    y_ssd = _ssd(r3(xbc), r3(dt), r3(z), p["conv_w"], p["conv_b"], p["dt_bias"], p["a_lane"],
                 p["d_skip"], row(g_ssd))
    y_att = _attention(r3(q), r3(k), r3(v), p["t2"], row(g_att))
    x1, u, aff_t = _out_proj(x2d, y_ssd.reshape(t, D_SSD), y_att.reshape(t, D_ATT),
                             p["w_out"], row(g_ffn), p["w_router"])
    sel, offs, cnt = _route(aff_t)
    xe = _dispatch(u, sel, aff_t, offs, cnt)
    oe = _ffn(xe, *experts, layer, CAP_FACTOR * t // N_EXPERTS)
    return _combine(x1, sel, oe, offs, cnt, row(g_final), final_norm)


def _trunk(x, layers, experts, norm_mix_g, ssd_norm_g, attn_norm_g, norm_ffn_g, norm_final_g):
    b, l, _ = x.shape
    x2d = x.reshape(b * l, D_MODEL)
    depth = len(layers)
    for i, p in enumerate(layers):
        x2d = _layer(x2d, b, l, p, experts, i, norm_mix_g[i], ssd_norm_g[i], attn_norm_g[i],
                     norm_ffn_g[i], norm_final_g, i == depth - 1)
    return x2d.reshape(b, l, D_MODEL)


def kernel(x_prompt, x_sample, norm_mix_g, w_in, conv_w, conv_b, dt_bias, a_log, d_skip, ssd_norm_g, attn_norm_g, rpb, w_out, norm_ffn_g, w_router, w_gate, w_up, w_down, norm_final_g):
    depth = w_in.shape[0]
    layers = [_prep_layer(w_in[i], conv_w[i], conv_b[i], dt_bias[i], a_log[i], d_skip[i], rpb[i],
                          w_out[i], w_router[i])
              for i in range(depth)]
    experts = (w_gate.astype(BF16), w_up.astype(BF16), w_down.astype(BF16))
    run = functools.partial(_trunk, layers=layers, experts=experts,
                            norm_mix_g=norm_mix_g, ssd_norm_g=ssd_norm_g,
                            attn_norm_g=attn_norm_g, norm_ffn_g=norm_ffn_g,
                            norm_final_g=norm_final_g)
    return (run(x_prompt), run(x_sample))
```

```python
import functools

import jax
import jax.numpy as jnp
from jax import lax
from jax.experimental import pallas as pl
from jax.experimental.pallas import tpu as pltpu

F32 = jnp.float32
BF16 = jnp.bfloat16
U32 = jnp.uint32
I32 = jnp.int32

D_MODEL = 1024
D_SSD = 512
D_ATT = 512
N_HEADS = 8
CONV_W = 5
CONV_DIM = 768
CHUNK = 128
GRID_W = 64
KH = 8
KW = 16
N_EXPERTS = 16
CAP_FACTOR = 2
D_FF = 2048
RMS_EPS = 1e-6
NEG = -1e30

LANES = 128
HALO = 16
IN_COLS_PADDED = 2944

ROW_TILE = 512
ATT_ROWS = 8
ATT_KROWS = 16
ATT_VISIT = 5
TOK_BLOCK = 512
SLOT_WIN = 128
SLOT_STEP = 120
SUBLANES = 8
FFN_TILE = 512
VMEM_LIMIT = 56 * 1024 * 1024


def _cparams(sem):
    return pltpu.CompilerParams(dimension_semantics=sem, vmem_limit_bytes=VMEM_LIMIT)


def _sigmoid(x):
    return 1.0 / (1.0 + jnp.exp(-x))


def _silu(x):
    return x * _sigmoid(x)


def _softplus(x):
    return jnp.maximum(x, 0.0) + jnp.log(1.0 + jnp.exp(-jnp.abs(x)))


def _rms(x, g):
    return x * lax.rsqrt(jnp.mean(x * x, axis=-1, keepdims=True) + RMS_EPS) * g


def _bits(x):
    return lax.bitcast_convert_type(x, U32)


def _pack_halves(x):
    lo = _bits(x[:, :512]) >> 16
    hi = _bits(x[:, 512:]) & jnp.uint32(0xFFFF0000)
    return hi | lo


def _unpack_halves(p):
    lo = lax.bitcast_convert_type(p << 16, F32)
    hi = lax.bitcast_convert_type(p & jnp.uint32(0xFFFF0000), F32)
    return jnp.concatenate([lo, hi], axis=1).astype(BF16)


def _in_proj_kernel(x_ref, g_ref, w_ref, z_ref, xbc_ref, q_ref, k_ref, v_ref, dt_ref):
    u = _rms(x_ref[...], g_ref[...]).astype(BF16)

    def proj(a, b):
        return jnp.dot(u, w_ref[:, a:b], preferred_element_type=F32)

    z_ref[...] = proj(0, 512).astype(BF16)
    xbc_ref[...] = proj(512, 1280).astype(BF16)
    q_ref[...] = (proj(1280, 1792) * (64 ** -0.5)).astype(BF16)
    k_ref[...] = proj(1792, 2304).astype(BF16)
    v_ref[...] = proj(2304, 2816).astype(BF16)
    dt_ref[...] = proj(2816, 2944)


def _in_proj(x2d, g, w):
    t = x2d.shape[0]
    tm = min(ROW_TILE, t)
    row = lambda n: pl.BlockSpec((tm, n), lambda i: (i, 0))
    full = lambda a: pl.BlockSpec(a.shape, lambda i: (0, 0))
    return pl.pallas_call(
        _in_proj_kernel,
        grid=(t // tm,),
        in_specs=[row(D_MODEL), full(g), full(w)],
        out_specs=[row(512), row(CONV_DIM), row(512), row(512), row(512), row(LANES)],
        out_shape=[jax.ShapeDtypeStruct((t, 512), BF16),
                   jax.ShapeDtypeStruct((t, CONV_DIM), BF16),
                   jax.ShapeDtypeStruct((t, 512), BF16),
                   jax.ShapeDtypeStruct((t, 512), BF16),
                   jax.ShapeDtypeStruct((t, 512), BF16),
                   jax.ShapeDtypeStruct((t, LANES), F32)],
        compiler_params=_cparams(("parallel",)),
    )(x2d, g, w)


def _conv_silu(xc_ref, xp_ref, xn_ref, cw_ref, cb_ref, c, nc):
    cur = xc_ref[0].astype(F32)
    prev = jnp.where(c > 0, xp_ref[0].astype(F32), 0.0)
    nxt = jnp.where(c < nc - 1, xn_ref[0].astype(F32), 0.0)
    xpad = jnp.concatenate([prev, cur, nxt], axis=0)
    acc = jnp.broadcast_to(cb_ref[...], (CHUNK, CONV_DIM))
    for k in range(CONV_W):
        start = HALO - CONV_W // 2 + k
        acc = acc + cw_ref[k:k + 1, :] * xpad[start:start + CHUNK, :]
    return _silu(acc)


def _decay_terms(dt_ref, dtb_ref, a_ref):
    dtv = _softplus(dt_ref[0] + dtb_ref[...])
    a = dtv * a_ref[...]
    li = lax.broadcasted_iota(I32, (CHUNK, CHUNK), 0)
    si = lax.broadcasted_iota(I32, (CHUNK, CHUNK), 1)
    tri = jnp.where(si <= li, 1.0, 0.0).astype(F32)
    cs = jnp.dot(tri, a, preferred_element_type=F32, precision=lax.Precision.HIGHEST)
    tot = cs[CHUNK - 1:CHUNK, :]
    return dtv, cs, cs - a, tot


def _pair_cols(x, h0, left):
    return jnp.where(left, x[:, h0:h0 + 1], x[:, h0 + 1:h0 + 2])


def _ssd_bwd_state_kernel(xc_ref, xp_ref, xn_ref, dt_ref, cw_ref, cb_ref, dtb_ref, a_ref,
                          sin_ref, act_ref, sb_ref, *, nc):
    i = pl.program_id(1)
    c = nc - 1 - i

    @pl.when(i == 0)
    def _():
        sb_ref[...] = jnp.zeros_like(sb_ref)

    sin_ref[0, 0] = sb_ref[...]
    act = _conv_silu(xc_ref, xp_ref, xn_ref, cw_ref, cb_ref, c, nc)
    act_ref[0] = act.astype(BF16)
    xs = act[:, :D_SSD]
    bt = act[:, D_SSD:D_SSD + LANES].T.astype(BF16)
    dtv, _, ex, tot = _decay_terms(dt_ref, dtb_ref, a_ref)
    wb = jnp.exp(ex) * dtv
    decb = jnp.exp(tot)
    left = lax.broadcasted_iota(I32, (1, LANES), 1) < 64
    for j in range(4):
        xw = (xs[:, LANES * j:LANES * (j + 1)] * _pair_cols(wb, 8 + 2 * j, left)).astype(BF16)
        st = jnp.dot(bt, xw, preferred_element_type=F32)
        sb_ref[j] = _pair_cols(decb, 8 + 2 * j, left) * sb_ref[j] + st


def _ssd_main_kernel(act_ref, dt_ref, z_ref, sin_ref, dtb_ref, a_ref, dskip_ref, g_ref, o_ref,
                     sf_ref, y_ref):
    c = pl.program_id(1)

    @pl.when(c == 0)
    def _():
        sf_ref[...] = jnp.zeros_like(sf_ref)

    bm_b = act_ref[0, :, D_SSD:D_SSD + LANES]
    cm_b = act_ref[0, :, D_SSD + LANES:]
    dtv, cs, ex, tot = _decay_terms(dt_ref, dtb_ref, a_ref)
    cs_t, ex_t, dt_t = cs.T, ex.T, dtv.T
    wf = jnp.exp(tot - cs) * dtv
    ef = jnp.exp(cs)
    eb = jnp.exp(tot - ex)
    decf = jnp.exp(tot)

    lane = lax.broadcasted_iota(I32, (1, LANES), 1)
    left = lane < 64
    li = lax.broadcasted_iota(I32, (CHUNK, CHUNK), 0)
    si = lax.broadcasted_iota(I32, (CHUNK, CHUNK), 1)
    bt = bm_b.astype(F32).T.astype(BF16)
    zero = jnp.zeros_like(bm_b)
    for grp in range(2):
        in_grp = left if grp == 0 else jnp.logical_not(left)
        b_g = jnp.where(in_grp, bm_b, zero)
        c_g = jnp.where(in_grp, cm_b, zero)
        cb_g = lax.dot_general(cm_b, b_g, (((1,), (1,)), ((), ())),
                               preferred_element_type=F32)
        for jj in range(2):
            j = 2 * grp + jj
            x_pb = act_ref[0, :, LANES * j:LANES * (j + 1)]
            x_p = x_pb.astype(F32)
            halves = []
            for sub in range(2):
                h = 2 * j + sub
                dtf = dt_t[h:h + 1, :]
                dtb = dt_t[8 + h:9 + h, :]
                arg = jnp.where(si <= li,
                                cs[:, h:h + 1] - cs_t[h:h + 1, :],
                                ex_t[8 + h:9 + h, :] - ex[:, 8 + h:9 + h])
                dsel = jnp.where(si < li, dtf, jnp.where(si > li, dtb, dtf + dtb))
                m = (cb_g * jnp.exp(arg) * dsel).astype(BF16)
                halves.append(jnp.dot(m, x_pb, preferred_element_type=F32))
            y = jnp.where(left, halves[0], halves[1])
            sf = sf_ref[j]
            sb = sin_ref[0, 0, j]
            y = y + jnp.dot(c_g, sf.astype(BF16), preferred_element_type=F32) \
                * _pair_cols(ef, 2 * j, left)
            y = y + jnp.dot(c_g, sb.astype(BF16), preferred_element_type=F32) \
                * _pair_cols(eb, 8 + 2 * j, left)
            y_ref[:, LANES * j:LANES * (j + 1)] = y + dskip_ref[:, LANES * j:LANES * (j + 1)] * x_p
            xw = (x_p * _pair_cols(wf, 2 * j, left)).astype(BF16)
            st = jnp.dot(bt, xw, preferred_element_type=F32)
            sf_ref[j] = _pair_cols(decf, 2 * j, left) * sf + st

    yz = y_ref[...] * _silu(z_ref[0].astype(F32))
    o_ref[0] = _rms(yz, g_ref[...]).astype(BF16)


def _ssd(xbc, dt, z, cw, cb, dtb, a_lane, dskip, g):
    b, l, _ = xbc.shape
    nc = l // CHUNK
    hb = CHUNK // HALO
    last_halo = l // HALO - 1

    def specs(chunk_of):
        cur = pl.BlockSpec((1, CHUNK, CONV_DIM), lambda bi, i: (bi, chunk_of(i), 0))
        prev = pl.BlockSpec((1, HALO, CONV_DIM),
                            lambda bi, i: (bi, jnp.maximum(chunk_of(i) * hb - 1, 0), 0))
        nxt = pl.BlockSpec((1, HALO, CONV_DIM),
                           lambda bi, i: (bi, jnp.minimum((chunk_of(i) + 1) * hb, last_halo), 0))
        dts = pl.BlockSpec((1, CHUNK, LANES), lambda bi, i: (bi, chunk_of(i), 0))
        return cur, prev, nxt, dts

    const = lambda a: pl.BlockSpec(a.shape, lambda bi, i: (0,) * a.ndim)

    rev = lambda i: nc - 1 - i
    cur, prev, nxt, dts = specs(rev)
    s_in, act = pl.pallas_call(
        functools.partial(_ssd_bwd_state_kernel, nc=nc),
        grid=(b, nc),
        in_specs=[cur, prev, nxt, dts, const(cw), const(cb), const(dtb), const(a_lane)],
        out_specs=[pl.BlockSpec((1, 1, 4, LANES, LANES), lambda bi, i: (bi, rev(i), 0, 0, 0)),
                   pl.BlockSpec((1, CHUNK, CONV_DIM), lambda bi, i: (bi, rev(i), 0))],
        out_shape=[jax.ShapeDtypeStruct((b, nc, 4, LANES, LANES), F32),
                   jax.ShapeDtypeStruct((b, l, CONV_DIM), BF16)],
        scratch_shapes=[pltpu.VMEM((4, LANES, LANES), F32)],
        compiler_params=_cparams(("parallel", "arbitrary")),
    )(xbc, xbc, xbc, dt, cw, cb, dtb, a_lane)

    chunk = lambda n: pl.BlockSpec((1, CHUNK, n), lambda bi, i: (bi, i, 0))
    return pl.pallas_call(
        _ssd_main_kernel,
        grid=(b, nc),
        in_specs=[chunk(CONV_DIM), chunk(LANES), chunk(D_SSD),
                  pl.BlockSpec((1, 1, 4, LANES, LANES), lambda bi, i: (bi, i, 0, 0, 0)),
                  const(dtb), const(a_lane), const(dskip), const(g)],
        out_specs=chunk(D_SSD),
        out_shape=jax.ShapeDtypeStruct((b, l, D_SSD), BF16),
        scratch_shapes=[pltpu.VMEM((4, LANES, LANES), F32), pltpu.VMEM((CHUNK, D_SSD), F32)],
        compiler_params=_cparams(("parallel", "arbitrary")),
    )(act, dt, z, s_in, dtb, a_lane, dskip, g)


def _att_kernel(q_ref, k0_ref, k1_ref, k2_ref, k3_ref, v0_ref, v1_ref, v2_ref, v3_ref,
                t2_ref, g_ref, o_ref, s_ref, t_ref, p_ref, linv_ref, acc_ref, *, rows):
    blk = pl.program_id(1)
    lane = lax.broadcasted_iota(I32, (1, LANES), 1)
    left = lane < 64
    n_pairs = ATT_KROWS // 2
    zero_tile = jnp.zeros((GRID_W, LANES), BF16)

    def softmax_tail(par, rq, mx, kcols_of, n_tiles):
        m = jnp.max(mx, axis=1, keepdims=True)
        lsum = jnp.zeros((GRID_W, LANES), F32)
        for d in range(n_tiles):
            p = jnp.exp(t_ref[par, rq, LANES * d:LANES * (d + 1)] - m)
            lsum = lsum + p
            p_ref[par, rq, kcols_of(d)] = p.astype(BF16)
        tot = jnp.sum(lsum, axis=1, keepdims=True)
        linv_ref[par, rq, :] = jnp.broadcast_to(1.0 / tot, (GRID_W, LANES))

    def softmax_row(i, h, par, place):
        koff = {"top": 0, "interior": -(KH // 2), "bottom": -KH}[place]
        js = {"top": max(i - KH // 2, 0), "interior": i, "bottom": min(i + KH // 2, KH)}[place]
        rq = slice(GRID_W * i, GRID_W * (i + 1))
        first, last = js // 2, (js + KH - 1) // 2
        kcols_of = lambda d: slice(LANES * (first + d), LANES * (first + d + 1))
        mx = None
        for d in range(last - first + 1):
            jl = 2 * (first + d)
            t = s_ref[par, rq, kcols_of(d)] + t2_ref[h, koff + jl - i + KH]
            if jl < js:
                t = jnp.where(left, NEG, t)
            if jl + 1 > js + KH - 1:
                t = jnp.where(left, t, NEG)
            t_ref[par, rq, LANES * d:LANES * (d + 1)] = t
            mx = t if mx is None else jnp.maximum(mx, t)
        softmax_tail(par, rq, mx, kcols_of, last - first + 1)
        for jp in range(n_pairs):
            if jp < first or jp > last:
                p_ref[par, rq, LANES * jp:LANES * (jp + 1)] = zero_tile

    def all_heads(place):
        for hp in range(N_HEADS // 2):
            cols = slice(LANES * hp, LANES * (hp + 1))
            qp = q_ref[0, :, cols]
            kp = jnp.concatenate([r[0, :, cols] for r in (k0_ref, k1_ref, k2_ref, k3_ref)], axis=0)
            vp = jnp.concatenate([r[0, :, cols] for r in (v0_ref, v1_ref, v2_ref, v3_ref)], axis=0)
            o_pair = None
            for sub in range(2):
                h = 2 * hp + sub
                in_head = left if sub == 0 else jnp.logical_not(left)
                km = jnp.where(in_head, kp, jnp.zeros_like(kp))
                s_ref[sub] = lax.dot_general(qp, km, (((1,), (1,)), ((), ())),
                                             preferred_element_type=F32)
                for i in range(ATT_ROWS):
                    softmax_row(i, h, sub, place)
                o_full = jnp.dot(p_ref[sub], vp, preferred_element_type=F32) * linv_ref[sub]
                o_pair = o_full if sub == 0 else jnp.where(left, o_pair, o_full)
            acc_ref[:, cols] = o_pair

    last_blk = rows // ATT_ROWS - 1
    pl.when(blk == 0)(lambda: all_heads("top"))
    pl.when(jnp.logical_and(blk > 0, blk < last_blk))(lambda: all_heads("interior"))
    pl.when(blk == last_blk)(lambda: all_heads("bottom"))
    o_ref[0] = _rms(acc_ref[...], g_ref[...]).astype(BF16)


def _attention(q, k, v, t2, g):
    b, l, _ = q.shape
    rows = l // GRID_W
    assert rows >= ATT_KROWS and rows % ATT_ROWS == 0
    nblk = rows // ATT_ROWS
    nq = ATT_ROWS * GRID_W
    kb = 4 * GRID_W
    n_kb = l // kb

    def kspec(m):
        return pl.BlockSpec(
            (1, kb, D_ATT),
            lambda bi, i, m=m: (bi, jnp.clip(2 * i - 1, 0, n_kb - 4) + m, 0))

    const = lambda a: pl.BlockSpec(a.shape, lambda bi, i: (0,) * a.ndim)
    return pl.pallas_call(
        functools.partial(_att_kernel, rows=rows),
        grid=(b, nblk),
        in_specs=[pl.BlockSpec((1, nq, D_ATT), lambda bi, i: (bi, i, 0))]
                 + [kspec(m) for m in range(4)] + [kspec(m) for m in range(4)]
                 + [const(t2), const(g)],
        out_specs=pl.BlockSpec((1, nq, D_ATT), lambda bi, i: (bi, i, 0)),
        out_shape=jax.ShapeDtypeStruct((b, l, D_ATT), BF16),
        scratch_shapes=[pltpu.VMEM((2, nq, ATT_KROWS * GRID_W), F32),
                        pltpu.VMEM((2, nq, ATT_VISIT * LANES), F32),
                        pltpu.VMEM((2, nq, ATT_KROWS * GRID_W), BF16),
                        pltpu.VMEM((2, nq, LANES), F32),
                        pltpu.VMEM((nq, D_ATT), F32)],
        compiler_params=_cparams(("parallel", "parallel")),
    )(q, k, k, k, k, v, v, v, v, t2, g)


def _bias_tables(rpb):
    c = jnp.arange(GRID_W)
    col_start = jnp.clip(c - KW // 2, 0, GRID_W - KW)
    col_mask = (c[None, :] >= col_start[:, None]) & (c[None, :] < col_start[:, None] + KW)
    dc = jnp.clip(c[None, :] - c[:, None] + (KW - 1), 0, 2 * KW - 2)
    t = rpb.astype(F32)[:, :, dc]
    t = jnp.where(col_mask[None, None], t, NEG)
    pad = jnp.full((N_HEADS, 1, GRID_W, GRID_W), NEG, F32)
    t = jnp.concatenate([pad, t, pad], axis=1)
    return jnp.concatenate([t[:, :-1], t[:, 1:]], axis=-1)


def _out_proj_kernel(x_ref, ys_ref, ya_ref, wo_ref, g_ref, wr_ref, x1_ref, u_ref, aff_ref):
    x1 = x_ref[...] \
        + jnp.dot(ys_ref[...], wo_ref[:D_SSD, :], preferred_element_type=F32) \
        + jnp.dot(ya_ref[...], wo_ref[D_SSD:, :], preferred_element_type=F32)
    x1_ref[...] = x1
    ub = _rms(x1, g_ref[...]).astype(BF16)
    u_ref[...] = ub
    logits = jnp.dot(ub, wr_ref[...], preferred_element_type=F32)
    lt = logits.T[:N_EXPERTS, :]
    e = jnp.exp(lt - jnp.max(lt, axis=0, keepdims=True))
    aff_ref[...] = e / jnp.sum(e, axis=0, keepdims=True)


def _out_proj(x2d, ys, ya, wo, g, wr):
    t = x2d.shape[0]
    tm = min(ROW_TILE, t)
    row = lambda n: pl.BlockSpec((tm, n), lambda i: (i, 0))
    full = lambda a: pl.BlockSpec(a.shape, lambda i: (0, 0))
    return pl.pallas_call(
        _out_proj_kernel,
        grid=(t // tm,),
        in_specs=[row(D_MODEL), row(D_SSD), row(D_ATT), full(wo), full(g), full(wr)],
        out_specs=[row(D_MODEL), row(D_MODEL), pl.BlockSpec((N_EXPERTS, tm), lambda i: (0, i))],
        out_shape=[jax.ShapeDtypeStruct((t, D_MODEL), F32),
                   jax.ShapeDtypeStruct((t, D_MODEL), BF16),
                   jax.ShapeDtypeStruct((N_EXPERTS, t), F32)],
        compiler_params=_cparams(("parallel",)),
    )(x2d, ys, ya, wo, g, wr)


def _route_kernel(aff_ref, sel_ref, offs_ref, cnt_ref, *, cap, nb):
    aff = aff_ref[...]
    t = aff.shape[1]
    keys = lax.bitcast_convert_type(aff, I32)
    capf = jnp.float32(cap)

    def count(mask):
        return jnp.sum(jnp.where(mask, 1.0, 0.0), axis=1, keepdims=True)

    def key_step(i, cur):
        cand = cur | lax.shift_left(jnp.int32(1), 30 - i)
        return jnp.where(count(keys >= cand) >= capf, cand, cur)

    thr = lax.fori_loop(0, 31, key_step, jnp.zeros((N_EXPERTS, 1), I32))
    gt = keys > thr
    eq = keys == thr
    need = capf - count(gt)
    idx = lax.broadcasted_iota(I32, (N_EXPERTS, t), 1)
    nbits = max(t.bit_length(), 1)

    def idx_step(i, cur):
        cand = cur | lax.shift_left(jnp.int32(1), nbits - 1 - i)
        return jnp.where(count(jnp.logical_and(eq, idx < cand)) < need, cand, cur)

    last = lax.fori_loop(0, nbits, idx_step, jnp.zeros((N_EXPERTS, 1), I32))
    sel = jnp.where(gt, 1.0, jnp.where(jnp.logical_and(eq, idx <= last), 1.0, 0.0))
    sel_ref[...] = sel

    lane = lax.broadcasted_iota(I32, (N_EXPERTS, LANES), 1)
    cnt = jnp.zeros((N_EXPERTS, LANES), F32)
    for b in range(nb):
        cb = jnp.sum(sel[:, TOK_BLOCK * b:TOK_BLOCK * (b + 1)], axis=1, keepdims=True)
        cnt = jnp.where(lane == b, cb, cnt)
    incl = cnt
    sh = 1
    while sh < LANES:
        incl = incl + jnp.where(lane >= sh, pltpu.roll(incl, sh, axis=1), 0.0)
        sh *= 2
    offs_ref[...] = (incl - cnt).astype(I32)
    cnt_ref[...] = cnt.astype(I32)


def _route(aff_t):
    t = aff_t.shape[1]
    cap = CAP_FACTOR * t // N_EXPERTS
    nb = t // TOK_BLOCK
    assert nb <= LANES
    return pl.pallas_call(
        functools.partial(_route_kernel, cap=cap, nb=nb),
        out_shape=[jax.ShapeDtypeStruct((N_EXPERTS, t), F32),
                   jax.ShapeDtypeStruct((N_EXPERTS, LANES), I32),
                   jax.ShapeDtypeStruct((N_EXPERTS, LANES), I32)],
        compiler_params=pltpu.CompilerParams(vmem_limit_bytes=VMEM_LIMIT),
    )(aff_t)


def _slot_positions(sel):
    si = lax.broadcasted_iota(I32, (TOK_BLOCK, TOK_BLOCK), 0)
    ti = lax.broadcasted_iota(I32, (TOK_BLOCK, TOK_BLOCK), 1)
    tri = jnp.where(si <= ti, 1.0, 0.0).astype(BF16)
    incl = jnp.dot(sel.astype(BF16), tri, preferred_element_type=F32)
    return jnp.where(sel > 0.0, incl - 1.0, -1e6)


def _num_chunks(cnt_ref, b):
    mx = cnt_ref[0, b]
    for e in range(1, N_EXPERTS):
        mx = jnp.maximum(mx, cnt_ref[e, b])
    return (mx + SLOT_STEP - 1) // SLOT_STEP


def _align_down(v):
    return pl.multiple_of((v // SUBLANES) * SUBLANES, SUBLANES)


def _dispatch_kernel(offs_ref, cnt_ref, u_ref, sel_ref, aff_ref, xe_hbm,
                     g_ref, stage_ref, carry_ref, sem, n_ref, *, nb, cap):
    b = pl.program_id(0)

    def copy(slot, e, row):
        return pltpu.make_async_copy(stage_ref.at[slot, e],
                                     xe_hbm.at[e, pl.ds(row, SLOT_WIN), :], sem.at[slot])

    @pl.when(b == 0)
    def _():
        n_ref[0] = 0
        carry_ref[...] = jnp.zeros_like(carry_ref)
        stage_ref[1, 0] = jnp.zeros((SLOT_WIN, 640), U32)
        for e in range(N_EXPERTS):
            pltpu.make_async_copy(stage_ref.at[1, 0], xe_hbm.at[e, pl.ds(cap, SLOT_WIN), :],
                                  sem.at[1]).start()
        for e in range(N_EXPERTS):
            copy(1, 0, 0).wait()

    posm = _slot_positions(sel_ref[...])
    aff = aff_ref[...]
    ub = u_ref[...]
    w_iota = lax.broadcasted_iota(I32, (SLOT_WIN, TOK_BLOCK), 0).astype(F32)
    head_row = lax.broadcasted_iota(I32, (SUBLANES, 640), 0)

    def chunk(k, carry):
        n = n_ref[0]
        slot = lax.rem(n, 2)
        win = []
        for e in range(N_EXPERTS):
            cnt = cnt_ref[e, b]
            done = jnp.minimum(k * SLOT_STEP, cnt)
            base = offs_ref[e, b] + done
            row = _align_down(base)
            end = base + jnp.minimum((k + 1) * SLOT_STEP, cnt) - done
            win.append((row, base - row, _align_down(end) - row))
            ge = posm[e:e + 1, :] == w_iota + (done - (base - row)).astype(F32)
            g_ref[SLOT_WIN * e:SLOT_WIN * (e + 1), :] = jnp.where(ge, 1.0, 0.0).astype(BF16)
            gate = jnp.sum(jnp.where(ge, aff[e:e + 1, :], 0.0), axis=1, keepdims=True)
            stage_ref[slot, e, :, 512:] = _bits(jnp.broadcast_to(gate, (SLOT_WIN, LANES)))
        packed = _pack_halves(jnp.dot(g_ref[...], ub, preferred_element_type=F32))
        for e in range(N_EXPERTS):
            row, skew, nxt = win[e]
            stage_ref[slot, e, :, :512] = packed[SLOT_WIN * e:SLOT_WIN * (e + 1), :]
            stage_ref[slot, e, :SUBLANES, :] = jnp.where(
                head_row < skew, carry_ref[e], stage_ref[slot, e, :SUBLANES, :])
            carry_ref[e] = stage_ref[slot, e, pl.ds(pl.multiple_of(nxt, SUBLANES), SUBLANES), :]

        @pl.when(n > 0)
        def _():
            for e in range(N_EXPERTS):
                copy(1 - slot, e, 0).wait()

        for e in range(N_EXPERTS):
            copy(slot, e, win[e][0]).start()
        n_ref[0] = n + 1
        return carry

    lax.fori_loop(0, _num_chunks(cnt_ref, b), chunk, 0)

    @pl.when(jnp.logical_and(b == nb - 1, n_ref[0] > 0))
    def _():
        slot = lax.rem(n_ref[0] - 1, 2)
        for e in range(N_EXPERTS):
            copy(slot, e, 0).wait()


def _dispatch(u, sel, aff_t, offs, cnt):
    t = u.shape[0]
    cap = CAP_FACTOR * t // N_EXPERTS
    nb = t // TOK_BLOCK
    cap_p = cap + SLOT_WIN
    grid_spec = pltpu.PrefetchScalarGridSpec(
        num_scalar_prefetch=2,
        grid=(nb,),
        in_specs=[pl.BlockSpec((TOK_BLOCK, D_MODEL), lambda b, *_: (b, 0)),
                  pl.BlockSpec((N_EXPERTS, TOK_BLOCK), lambda b, *_: (0, b)),
                  pl.BlockSpec((N_EXPERTS, TOK_BLOCK), lambda b, *_: (0, b))],
        out_specs=pl.BlockSpec(memory_space=pl.ANY),
        scratch_shapes=[pltpu.VMEM((N_EXPERTS * SLOT_WIN, TOK_BLOCK), BF16),
                        pltpu.VMEM((2, N_EXPERTS, SLOT_WIN, 640), U32),
                        pltpu.VMEM((N_EXPERTS, SUBLANES, 640), U32),
                        pltpu.SemaphoreType.DMA((2,)),
                        pltpu.SMEM((1,), I32)])
    return pl.pallas_call(
        functools.partial(_dispatch_kernel, nb=nb, cap=cap),
        grid_spec=grid_spec,
        out_shape=jax.ShapeDtypeStruct((N_EXPERTS, cap_p, 640), U32),
        compiler_params=_cparams(("arbitrary",)),
    )(offs, cnt, u, sel, aff_t)


def _ffn_kernel(xe_ref, wg_ref, wu_ref, wd_ref, o_ref):
    xp = xe_ref[0]
    x = _unpack_halves(xp[:, :512])
    gate = lax.bitcast_convert_type(xp[:, 512:513], F32)
    gp = jnp.dot(x, wg_ref[0, 0], preferred_element_type=F32)
    up = jnp.dot(x, wu_ref[0, 0], preferred_element_type=F32)
    hid = (_silu(gp) * up).astype(BF16)
    out = jnp.dot(hid, wd_ref[0, 0], preferred_element_type=F32) * gate
    o_ref[0] = _pack_halves(out.astype(BF16).astype(F32))


def _ffn(xe, wg, wu, wd, layer, cap):
    tm = min(FFN_TILE, cap)
    return pl.pallas_call(
        _ffn_kernel,
        grid=(N_EXPERTS, cap // tm),
        in_specs=[pl.BlockSpec((1, tm, 640), lambda e, j: (e, j, 0)),
                  pl.BlockSpec((1, 1, D_MODEL, D_FF), lambda e, j: (layer, e, 0, 0)),
                  pl.BlockSpec((1, 1, D_MODEL, D_FF), lambda e, j: (layer, e, 0, 0)),
                  pl.BlockSpec((1, 1, D_FF, D_MODEL), lambda e, j: (layer, e, 0, 0))],
        out_specs=pl.BlockSpec((1, tm, 512), lambda e, j: (e, j, 0)),
        out_shape=jax.ShapeDtypeStruct((N_EXPERTS, cap, 512), U32),
        compiler_params=_cparams(("parallel", "parallel")),
    )(xe, wg, wu, wd)


def _combine_kernel(offs_ref, cnt_ref, x1_ref, sel_ref, gfin_ref, oe_hbm, o_ref,
                    g_ref, stage_ref, acc_ref, sem, *, nb, cap, final_norm):
    b = pl.program_id(0)
    w_iota = lax.broadcasted_iota(I32, (SLOT_WIN, TOK_BLOCK), 0).astype(F32)

    def window(bb, k, e):
        done = jnp.minimum(k * SLOT_STEP, cnt_ref[e, bb])
        base = offs_ref[e, bb] + done
        row = jnp.minimum(_align_down(base), cap - SLOT_WIN)
        return row, done - (base - row)

    def copy(slot, e, row):
        return pltpu.make_async_copy(oe_hbm.at[e, pl.ds(row, SLOT_WIN), :],
                                     stage_ref.at[slot, pl.ds(SLOT_WIN * e, SLOT_WIN), :],
                                     sem.at[slot])

    def fetch(bb, k, slot):
        for e in range(N_EXPERTS):
            copy(slot, e, pl.multiple_of(window(bb, k, e)[0], SUBLANES)).start()

    def wait(slot):
        for e in range(N_EXPERTS):
            copy(slot, e, 0).wait()

    @pl.when(b == 0)
    def _():
        fetch(0, 0, 0)

    @pl.when(b + 1 < nb)
    def _():
        fetch(b + 1, 0, lax.rem(b + 1, 2))

    pos_all = _slot_positions(sel_ref[...])

    def expand(slot, k):
        lo = (k * SLOT_STEP).astype(F32)
        posm = jnp.where(jnp.logical_and(pos_all >= lo, pos_all < lo + SLOT_STEP), pos_all, -1e6)
        for e in range(N_EXPERTS):
            wi = w_iota + window(b, k, e)[1].astype(F32)
            g_ref[SLOT_WIN * e:SLOT_WIN * (e + 1), :] = jnp.where(posm[e:e + 1, :] == wi, 1.0, 0.0)
        gt = g_ref[...].T.astype(BF16)
        slab = _unpack_halves(stage_ref[slot])
        return jnp.dot(gt, slab, preferred_element_type=F32)

    wait(lax.rem(b, 2))
    acc_ref[...] = x1_ref[...] + expand(lax.rem(b, 2), jnp.int32(0))

    def extra(k, carry):
        fetch(b, k, 2)
        wait(2)
        acc_ref[...] += expand(2, k)
        return carry

    lax.fori_loop(1, _num_chunks(cnt_ref, b), extra, 0)
    y = acc_ref[...]
    o_ref[...] = _rms(y, gfin_ref[...]) if final_norm else y


def _combine(x1, sel, oe, offs, cnt, gfin, final_norm):
    t = x1.shape[0]
    cap = CAP_FACTOR * t // N_EXPERTS
    nb = t // TOK_BLOCK
    grid_spec = pltpu.PrefetchScalarGridSpec(
        num_scalar_prefetch=2,
        grid=(nb,),
        in_specs=[pl.BlockSpec((TOK_BLOCK, D_MODEL), lambda b, *_: (b, 0)),
                  pl.BlockSpec((N_EXPERTS, TOK_BLOCK), lambda b, *_: (0, b)),
                  pl.BlockSpec((1, D_MODEL), lambda b, *_: (0, 0)),
                  pl.BlockSpec(memory_space=pl.ANY)],
        out_specs=pl.BlockSpec((TOK_BLOCK, D_MODEL), lambda b, *_: (b, 0)),
        scratch_shapes=[pltpu.VMEM((N_EXPERTS * SLOT_WIN, TOK_BLOCK), F32),
                        pltpu.VMEM((3, N_EXPERTS * SLOT_WIN, 512), U32),
                        pltpu.VMEM((TOK_BLOCK, D_MODEL), F32),
                        pltpu.SemaphoreType.DMA((3,))])
    return pl.pallas_call(
        functools.partial(_combine_kernel, nb=nb, cap=cap, final_norm=final_norm),
        grid_spec=grid_spec,
        out_shape=jax.ShapeDtypeStruct((t, D_MODEL), F32),
        compiler_params=_cparams(("arbitrary",)),
    )(offs, cnt, x1, sel, gfin, oe)


def _prep_layer(w_in, conv_w, conv_b, dt_bias, a_log, d_skip, rpb, w_out, w_router):
    o1 = D_SSD
    o2 = o1 + CONV_DIM
    o3 = o2 + 2 * N_HEADS
    w = jnp.concatenate(
        [w_in[:, :o2], w_in[:, o3:], w_in[:, o2:o3],
         jnp.zeros((D_MODEL, LANES - 2 * N_HEADS), w_in.dtype)], axis=1).astype(BF16)
    lane_pad = lambda v: jnp.concatenate([v.reshape(-1).astype(F32),
                                          jnp.zeros((LANES - 2 * N_HEADS,), F32)])[None, :]
    return dict(
        w_in=w,
        conv_w=jnp.concatenate([conv_w.astype(F32), jnp.zeros((8 - CONV_W, CONV_DIM), F32)], 0),
        conv_b=conv_b.astype(F32)[None, :],
        dt_bias=lane_pad(dt_bias),
        a_lane=lane_pad(-jnp.exp(a_log.astype(F32))),
        d_skip=jnp.repeat(d_skip.astype(F32), 64)[None, :],
        t2=_bias_tables(rpb),
        w_out=w_out.astype(BF16),
        w_router=jnp.concatenate(
            [w_router, jnp.zeros((D_MODEL, LANES - N_EXPERTS), w_router.dtype)], 1).astype(BF16))


def _layer(x2d, b, l, p, experts, layer, g_mix, g_ssd, g_att, g_ffn, g_final, final_norm):
    t = b * l
    row = lambda v: v.astype(F32)[None, :]
    z, xbc, q, k, v, dt = _in_proj(x2d, row(g_mix), p["w_in"])
    r3 = lambda a: a.reshape(b, l, a.shape[-1])
    y_ssd = _ssd(r3(xbc), r3(dt), r3(z), p["conv_w"], p["conv_b"], p["dt_bias"], p["a_lane"],
                 p["d_skip"], row(g_ssd))
    y_att = _attention(r3(q), r3(k), r3(v), p["t2"], row(g_att))
    x1, u, aff_t = _out_proj(x2d, y_ssd.reshape(t, D_SSD), y_att.reshape(t, D_ATT),
                             p["w_out"], row(g_ffn), p["w_router"])
    sel, offs, cnt = _route(aff_t)
    xe = _dispatch(u, sel, aff_t, offs, cnt)
    oe = _ffn(xe, *experts, layer, CAP_FACTOR * t // N_EXPERTS)
    return _combine(x1, sel, oe, offs, cnt, row(g_final), final_norm)


def _trunk(x, layers, experts, norm_mix_g, ssd_norm_g, attn_norm_g, norm_ffn_g, norm_final_g):
    b, l, _ = x.shape
    x2d = x.reshape(b * l, D_MODEL)
    depth = len(layers)
    for i, p in enumerate(layers):
        x2d = _layer(x2d, b, l, p, experts, i, norm_mix_g[i], ssd_norm_g[i], attn_norm_g[i],
                     norm_ffn_g[i], norm_final_g, i == depth - 1)
    return x2d.reshape(b, l, D_MODEL)


def kernel(x_prompt, x_sample, norm_mix_g, w_in, conv_w, conv_b, dt_bias, a_log, d_skip, ssd_norm_g, attn_norm_g, rpb, w_out, norm_ffn_g, w_router, w_gate, w_up, w_down, norm_final_g):
    depth = w_in.shape[0]
    layers = [_prep_layer(w_in[i], conv_w[i], conv_b[i], dt_bias[i], a_log[i], d_skip[i], rpb[i],
                          w_out[i], w_router[i])
              for i in range(depth)]
    experts = (w_gate.astype(BF16), w_up.astype(BF16), w_down.astype(BF16))
    run = functools.partial(_trunk, layers=layers, experts=experts,
                            norm_mix_g=norm_mix_g, ssd_norm_g=ssd_norm_g,
                            attn_norm_g=attn_norm_g, norm_ffn_g=norm_ffn_g,
                            norm_final_g=norm_final_g)
    return (run(x_prompt), run(x_sample))
```

```python
import functools

import jax
import jax.numpy as jnp
from jax import lax
from jax.experimental import pallas as pl
from jax.experimental.pallas import tpu as pltpu

F32 = jnp.float32
BF16 = jnp.bfloat16
U32 = jnp.uint32
I32 = jnp.int32

D_MODEL = 1024
D_SSD = 512
D_ATT = 512
N_HEADS = 8
CONV_W = 5
CONV_DIM = 768
CHUNK = 128
GRID_W = 64
KH = 8
KW = 16
N_EXPERTS = 16
CAP_FACTOR = 2
D_FF = 2048
RMS_EPS = 1e-6
NEG = -1e30

LANES = 128
HALO = 16
IN_COLS_PADDED = 2944

ROW_TILE = 1024
SSD_SEQS = 1
ATT_ROWS = 8
ATT_KROWS = 16
ATT_VISIT = 5
TOK_BLOCK = 512
SLOT_WIN = 128
SLOT_STEP = 120
SUBLANES = 8
FFN_TILE = 512
VMEM_LIMIT = 56 * 1024 * 1024


def _cparams(sem):
    return pltpu.CompilerParams(dimension_semantics=sem, vmem_limit_bytes=VMEM_LIMIT)


def _sigmoid(x):
    return 1.0 / (1.0 + jnp.exp(-x))


def _silu(x):
    return x * _sigmoid(x)


def _softplus(x):
    return jnp.maximum(x, 0.0) + jnp.log(1.0 + jnp.exp(-jnp.abs(x)))


def _rms(x, g):
    return x * lax.rsqrt(jnp.mean(x * x, axis=-1, keepdims=True) + RMS_EPS) * g


def _bits(x):
    return lax.bitcast_convert_type(x, U32)


def _pack_halves(x):
    lo = _bits(x[:, :512]) >> 16
    hi = _bits(x[:, 512:]) & jnp.uint32(0xFFFF0000)
    return hi | lo


def _unpack_halves(p):
    lo = lax.bitcast_convert_type(p << 16, F32)
    hi = lax.bitcast_convert_type(p & jnp.uint32(0xFFFF0000), F32)
    return jnp.concatenate([lo, hi], axis=1).astype(BF16)


def _in_proj_kernel(x_ref, g_ref, w_ref, z_ref, xbc_ref, q_ref, k_ref, v_ref, dt_ref):
    u = _rms(x_ref[...], g_ref[...]).astype(BF16)

    def proj(a, b):
        return jnp.dot(u, w_ref[:, a:b], preferred_element_type=F32)

    z_ref[...] = proj(0, 512).astype(BF16)
    xbc_ref[...] = proj(512, 1280).astype(BF16)
    q_ref[...] = (proj(1280, 1792) * (64 ** -0.5)).astype(BF16)
    k_ref[...] = proj(1792, 2304).astype(BF16)
    v_ref[...] = proj(2304, 2816).astype(BF16)
    dt_ref[...] = proj(2816, 2944)


def _in_proj(x2d, g, w):
    t = x2d.shape[0]
    tm = min(ROW_TILE, t)
    row = lambda n: pl.BlockSpec((tm, n), lambda i: (i, 0))
    full = lambda a: pl.BlockSpec(a.shape, lambda i: (0, 0))
    return pl.pallas_call(
        _in_proj_kernel,
        grid=(t // tm,),
        in_specs=[row(D_MODEL), full(g), full(w)],
        out_specs=[row(512), row(CONV_DIM), row(512), row(512), row(512), row(LANES)],
        out_shape=[jax.ShapeDtypeStruct((t, 512), BF16),
                   jax.ShapeDtypeStruct((t, CONV_DIM), BF16),
                   jax.ShapeDtypeStruct((t, 512), BF16),
                   jax.ShapeDtypeStruct((t, 512), BF16),
                   jax.ShapeDtypeStruct((t, 512), BF16),
                   jax.ShapeDtypeStruct((t, LANES), F32)],
        compiler_params=_cparams(("parallel",)),
    )(x2d, g, w)


def _conv_silu(xc_ref, xp_ref, xn_ref, cw_ref, cb_ref, c, nc, s):
    cur = xc_ref[s].astype(F32)
    prev = jnp.where(c > 0, xp_ref[s].astype(F32), 0.0)
    nxt = jnp.where(c < nc - 1, xn_ref[s].astype(F32), 0.0)
    xpad = jnp.concatenate([prev, cur, nxt], axis=0)
    acc = jnp.broadcast_to(cb_ref[...], (CHUNK, CONV_DIM))
    for k in range(CONV_W):
        start = HALO - CONV_W // 2 + k
        acc = acc + cw_ref[k:k + 1, :] * xpad[start:start + CHUNK, :]
    return _silu(acc)


def _decay_terms(dt_ref, dtb_ref, a_ref, s):
    dtv = _softplus(dt_ref[s] + dtb_ref[...])
    a = dtv * a_ref[...]
    li = lax.broadcasted_iota(I32, (CHUNK, CHUNK), 0)
    si = lax.broadcasted_iota(I32, (CHUNK, CHUNK), 1)
    tri = jnp.where(si <= li, 1.0, 0.0).astype(F32)
    cs = jnp.dot(tri, a, preferred_element_type=F32, precision=lax.Precision.HIGHEST)
    tot = cs[CHUNK - 1:CHUNK, :]
    return dtv, cs, cs - a, tot


def _pair_cols(x, h0, left):
    return jnp.where(left, x[:, h0:h0 + 1], x[:, h0 + 1:h0 + 2])


def _ssd_bwd_state_kernel(xc_ref, xp_ref, xn_ref, dt_ref, cw_ref, cb_ref, dtb_ref, a_ref,
                          sin_ref, act_ref, sb_ref, *, nc):
    i = pl.program_id(1)
    c = nc - 1 - i

    @pl.when(i == 0)
    def _():
        sb_ref[...] = jnp.zeros_like(sb_ref)

    left = lax.broadcasted_iota(I32, (1, LANES), 1) < 64
    for s in range(SSD_SEQS):
        sin_ref[s, 0] = sb_ref[s]
        act = _conv_silu(xc_ref, xp_ref, xn_ref, cw_ref, cb_ref, c, nc, s)
        act_ref[s] = act.astype(BF16)
        xs = act[:, :D_SSD]
        bt = act[:, D_SSD:D_SSD + LANES].T.astype(BF16)
        dtv, _, ex, tot = _decay_terms(dt_ref, dtb_ref, a_ref, s)
        wb = jnp.exp(ex) * dtv
        decb = jnp.exp(tot)
        for j in range(4):
            xw = (xs[:, LANES * j:LANES * (j + 1)] * _pair_cols(wb, 8 + 2 * j, left)).astype(BF16)
            st = jnp.dot(bt, xw, preferred_element_type=F32)
            sb_ref[s, j] = _pair_cols(decb, 8 + 2 * j, left) * sb_ref[s, j] + st


def _ssd_main_kernel(act_ref, dt_ref, z_ref, sin_ref, dtb_ref, a_ref, dskip_ref, g_ref, o_ref,
                     sf_ref, y_ref):
    c = pl.program_id(1)

    @pl.when(c == 0)
    def _():
        sf_ref[...] = jnp.zeros_like(sf_ref)

    lane = lax.broadcasted_iota(I32, (1, LANES), 1)
    left = lane < 64
    li = lax.broadcasted_iota(I32, (CHUNK, CHUNK), 0)
    si = lax.broadcasted_iota(I32, (CHUNK, CHUNK), 1)
    for s in range(SSD_SEQS):
        bm_b = act_ref[s, :, D_SSD:D_SSD + LANES]
        cm_b = act_ref[s, :, D_SSD + LANES:]
        dtv, cs, ex, tot = _decay_terms(dt_ref, dtb_ref, a_ref, s)
        cs_t, ex_t, dt_t = cs.T, ex.T, dtv.T
        wf = jnp.exp(tot - cs) * dtv
        ef = jnp.exp(cs)
        eb = jnp.exp(tot - ex)
        decf = jnp.exp(tot)
        bt = bm_b.astype(F32).T.astype(BF16)
        zero = jnp.zeros_like(bm_b)
        for grp in range(2):
            in_grp = left if grp == 0 else jnp.logical_not(left)
            b_g = jnp.where(in_grp, bm_b, zero)
            c_g = jnp.where(in_grp, cm_b, zero)
            cb_g = lax.dot_general(cm_b, b_g, (((1,), (1,)), ((), ())),
                                   preferred_element_type=F32)
            for jj in range(2):
                j = 2 * grp + jj
                cols = slice(LANES * j, LANES * (j + 1))
                x_pb = act_ref[s, :, cols]
                x_p = x_pb.astype(F32)
                halves = []
                for sub in range(2):
                    h = 2 * j + sub
                    dtf = dt_t[h:h + 1, :]
                    dtb = dt_t[8 + h:9 + h, :]
                    arg = jnp.where(si <= li,
                                    cs[:, h:h + 1] - cs_t[h:h + 1, :],
                                    ex_t[8 + h:9 + h, :] - ex[:, 8 + h:9 + h])
                    dsel = jnp.where(si < li, dtf, jnp.where(si > li, dtb, dtf + dtb))
                    m = (cb_g * jnp.exp(arg) * dsel).astype(BF16)
                    halves.append(jnp.dot(m, x_pb, preferred_element_type=F32))
                y = jnp.where(left, halves[0], halves[1])
                sf = sf_ref[s, j]
                sb = sin_ref[s, 0, j]
                y = y + jnp.dot(c_g, sf.astype(BF16), preferred_element_type=F32) \
                    * _pair_cols(ef, 2 * j, left)
                y = y + jnp.dot(c_g, sb.astype(BF16), preferred_element_type=F32) \
                    * _pair_cols(eb, 8 + 2 * j, left)
                y_ref[s, :, cols] = y + dskip_ref[:, cols] * x_p
                xw = (x_p * _pair_cols(wf, 2 * j, left)).astype(BF16)
                st = jnp.dot(bt, xw, preferred_element_type=F32)
                sf_ref[s, j] = _pair_cols(decf, 2 * j, left) * sf + st

        yz = y_ref[s] * _silu(z_ref[s].astype(F32))
        o_ref[s] = _rms(yz, g_ref[...]).astype(BF16)


def _ssd(xbc, dt, z, cw, cb, dtb, a_lane, dskip, g):
    b, l, _ = xbc.shape
    nc = l // CHUNK
    hb = CHUNK // HALO
    last_halo = l // HALO - 1

    assert b % SSD_SEQS == 0
    ns = SSD_SEQS
    rev = lambda i: nc - 1 - i
    const = lambda a: pl.BlockSpec(a.shape, lambda bi, i: (0,) * a.ndim)
    state = lambda chunk_of: pl.BlockSpec((ns, 1, 4, LANES, LANES),
                                          lambda bi, i: (bi, chunk_of(i), 0, 0, 0))
    chunk = lambda n, chunk_of: pl.BlockSpec((ns, CHUNK, n), lambda bi, i: (bi, chunk_of(i), 0))
    prev = pl.BlockSpec((ns, HALO, CONV_DIM),
                        lambda bi, i: (bi, jnp.maximum(rev(i) * hb - 1, 0), 0))
    nxt = pl.BlockSpec((ns, HALO, CONV_DIM),
                       lambda bi, i: (bi, jnp.minimum((rev(i) + 1) * hb, last_halo), 0))
    s_in, act = pl.pallas_call(
        functools.partial(_ssd_bwd_state_kernel, nc=nc),
        grid=(b // ns, nc),
        in_specs=[chunk(CONV_DIM, rev), prev, nxt, chunk(LANES, rev),
                  const(cw), const(cb), const(dtb), const(a_lane)],
        out_specs=[state(rev), chunk(CONV_DIM, rev)],
        out_shape=[jax.ShapeDtypeStruct((b, nc, 4, LANES, LANES), F32),
                   jax.ShapeDtypeStruct((b, l, CONV_DIM), BF16)],
        scratch_shapes=[pltpu.VMEM((ns, 4, LANES, LANES), F32)],
        compiler_params=_cparams(("parallel", "arbitrary")),
    )(xbc, xbc, xbc, dt, cw, cb, dtb, a_lane)

    fwd = lambda i: i
    return pl.pallas_call(
        _ssd_main_kernel,
        grid=(b // ns, nc),
        in_specs=[chunk(CONV_DIM, fwd), chunk(LANES, fwd), chunk(D_SSD, fwd), state(fwd),
                  const(dtb), const(a_lane), const(dskip), const(g)],
        out_specs=chunk(D_SSD, fwd),
        out_shape=jax.ShapeDtypeStruct((b, l, D_SSD), BF16),
        scratch_shapes=[pltpu.VMEM((ns, 4, LANES, LANES), F32),
                        pltpu.VMEM((ns, CHUNK, D_SSD), F32)],
        compiler_params=_cparams(("parallel", "arbitrary")),
    )(act, dt, z, s_in, dtb, a_lane, dskip, g)


def _att_kernel(q_ref, k0_ref, k1_ref, k2_ref, k3_ref, v0_ref, v1_ref, v2_ref, v3_ref,
                t2_ref, g_ref, o_ref, s_ref, t_ref, p_ref, linv_ref, ov_ref, acc_ref, *, rows):
    blk = pl.program_id(1)
    lane = lax.broadcasted_iota(I32, (1, LANES), 1)
    left = lane < 64
    n_pairs = ATT_KROWS // 2
    zero_tile = jnp.zeros((GRID_W, LANES), BF16)

    def softmax_tail(par, rq, mx, kcols_of, n_tiles):
        m = jnp.max(mx, axis=1, keepdims=True)
        lsum = jnp.zeros((GRID_W, LANES), F32)
        for d in range(n_tiles):
            p = jnp.exp(t_ref[par, rq, LANES * d:LANES * (d + 1)] - m)
            lsum = lsum + p
            p_ref[par, rq, kcols_of(d)] = p.astype(BF16)
        tot = jnp.sum(lsum, axis=1, keepdims=True)
        linv_ref[par, rq, :] = jnp.broadcast_to(1.0 / tot, (GRID_W, LANES))

    def window(i, place):
        koff = {"top": 0, "interior": -(KH // 2), "bottom": -KH}[place]
        js = {"top": max(i - KH // 2, 0), "interior": i, "bottom": min(i + KH // 2, KH)}[place]
        return koff, js, js // 2, (js + KH - 1) // 2

    def softmax_row(i, h, par, place):
        koff, js, first, last = window(i, place)
        rq = slice(GRID_W * i, GRID_W * (i + 1))
        kcols_of = lambda d: slice(LANES * (first + d), LANES * (first + d + 1))
        mx = None
        for d in range(last - first + 1):
            jl = 2 * (first + d)
            t = s_ref[par, rq, kcols_of(d)] + t2_ref[h, koff + jl - i + KH]
            if jl < js:
                t = jnp.where(left, NEG, t)
            if jl + 1 > js + KH - 1:
                t = jnp.where(left, t, NEG)
            t_ref[par, rq, LANES * d:LANES * (d + 1)] = t
            mx = t if mx is None else jnp.maximum(mx, t)
        softmax_tail(par, rq, mx, kcols_of, last - first + 1)
        for jp in range(2 * (first // 2), 2 * (last // 2) + 2):
            if jp < first or jp > last:
                p_ref[par, rq, LANES * jp:LANES * (jp + 1)] = zero_tile

    def all_heads(place):
        kt_rows = []
        for kt in range(n_pairs // 2):
            vis = [i for i in range(ATT_ROWS)
                   if window(i, place)[2] // 2 <= kt <= window(i, place)[3] // 2]
            if vis:
                kt_rows.append((kt, vis[0], vis[-1] + 1))
        k_refs = (k0_ref, k1_ref, k2_ref, k3_ref)
        v_refs = (v0_ref, v1_ref, v2_ref, v3_ref)
        for hp in range(N_HEADS // 2):
            cols = slice(LANES * hp, LANES * (hp + 1))
            o_pair = None
            for sub in range(2):
                h = 2 * hp + sub
                in_head = left if sub == 0 else jnp.logical_not(left)
                for kt, ra, rb in kt_rows:
                    keys = slice(2 * LANES * kt, 2 * LANES * (kt + 1))
                    k_tile = k_refs[kt][0, :, cols]
                    s_ref[sub, GRID_W * ra:GRID_W * rb, keys] = lax.dot_general(
                        q_ref[0, GRID_W * ra:GRID_W * rb, cols],
                        jnp.where(in_head, k_tile, jnp.zeros_like(k_tile)),
                        (((1,), (1,)), ((), ())), preferred_element_type=F32)
                for i in range(ATT_ROWS):
                    softmax_row(i, h, sub, place)
                done = 0
                for kt, ra, rb in kt_rows:
                    keys = slice(2 * LANES * kt, 2 * LANES * (kt + 1))
                    part = jnp.dot(p_ref[sub, GRID_W * ra:GRID_W * rb, keys],
                                   v_refs[kt][0, :, cols], preferred_element_type=F32)
                    old = max(min(rb, done) - ra, 0)
                    if old:
                        ov_ref[sub, GRID_W * ra:GRID_W * (ra + old), :] += part[:GRID_W * old]
                    if ra + old < rb:
                        ov_ref[sub, GRID_W * (ra + old):GRID_W * rb, :] = part[GRID_W * old:]
                    done = max(done, rb)
                o_full = ov_ref[sub] * linv_ref[sub]
                o_pair = o_full if sub == 0 else jnp.where(left, o_pair, o_full)
            acc_ref[:, cols] = o_pair

    last_blk = rows // ATT_ROWS - 1
    pl.when(blk == 0)(lambda: all_heads("top"))
    pl.when(jnp.logical_and(blk > 0, blk < last_blk))(lambda: all_heads("interior"))
    pl.when(blk == last_blk)(lambda: all_heads("bottom"))
    o_ref[0] = _rms(acc_ref[...], g_ref[...]).astype(BF16)


def _attention(q, k, v, t2, g):
    b, l, _ = q.shape
    rows = l // GRID_W
    assert rows >= ATT_KROWS and rows % ATT_ROWS == 0
    nblk = rows // ATT_ROWS
    nq = ATT_ROWS * GRID_W
    kb = 4 * GRID_W
    n_kb = l // kb

    def kspec(m):
        return pl.BlockSpec(
            (1, kb, D_ATT),
            lambda bi, i, m=m: (bi, jnp.clip(2 * i - 1, 0, n_kb - 4) + m, 0))

    const = lambda a: pl.BlockSpec(a.shape, lambda bi, i: (0,) * a.ndim)
    return pl.pallas_call(
        functools.partial(_att_kernel, rows=rows),
        grid=(b, nblk),
        in_specs=[pl.BlockSpec((1, nq, D_ATT), lambda bi, i: (bi, i, 0))]
                 + [kspec(m) for m in range(4)] + [kspec(m) for m in range(4)]
                 + [const(t2), const(g)],
        out_specs=pl.BlockSpec((1, nq, D_ATT), lambda bi, i: (bi, i, 0)),
        out_shape=jax.ShapeDtypeStruct((b, l, D_ATT), BF16),
        scratch_shapes=[pltpu.VMEM((2, nq, ATT_KROWS * GRID_W), F32),
                        pltpu.VMEM((2, nq, ATT_VISIT * LANES), F32),
                        pltpu.VMEM((2, nq, ATT_KROWS * GRID_W), BF16),
                        pltpu.VMEM((2, nq, LANES), F32),
                        pltpu.VMEM((2, nq, LANES), F32),
                        pltpu.VMEM((nq, D_ATT), F32)],
        compiler_params=_cparams(("parallel", "parallel")),
    )(q, k, k, k, k, v, v, v, v, t2, g)


def _bias_tables(rpb):
    c = jnp.arange(GRID_W)
    col_start = jnp.clip(c - KW // 2, 0, GRID_W - KW)
    col_mask = (c[None, :] >= col_start[:, None]) & (c[None, :] < col_start[:, None] + KW)
    dc = jnp.clip(c[None, :] - c[:, None] + (KW - 1), 0, 2 * KW - 2)
    t = rpb.astype(F32)[:, :, dc]
    t = jnp.where(col_mask[None, None], t, NEG)
    pad = jnp.full((N_HEADS, 1, GRID_W, GRID_W), NEG, F32)
    t = jnp.concatenate([pad, t, pad], axis=1)
    return jnp.concatenate([t[:, :-1], t[:, 1:]], axis=-1)


def _out_proj_kernel(x_ref, ys_ref, ya_ref, wo_ref, g_ref, wr_ref, x1_ref, u_ref, aff_ref):
    x1 = x_ref[...] \
        + jnp.dot(ys_ref[...], wo_ref[:D_SSD, :], preferred_element_type=F32) \
        + jnp.dot(ya_ref[...], wo_ref[D_SSD:, :], preferred_element_type=F32)
    x1_ref[...] = x1
    ub = _rms(x1, g_ref[...]).astype(BF16)
    u_ref[...] = ub
    logits = jnp.dot(ub, wr_ref[...], preferred_element_type=F32)
    lt = logits.T[:N_EXPERTS, :]
    e = jnp.exp(lt - jnp.max(lt, axis=0, keepdims=True))
    aff_ref[...] = e / jnp.sum(e, axis=0, keepdims=True)


def _out_proj(x2d, ys, ya, wo, g, wr):
    t = x2d.shape[0]
    tm = min(ROW_TILE, t)
    row = lambda n: pl.BlockSpec((tm, n), lambda i: (i, 0))
    full = lambda a: pl.BlockSpec(a.shape, lambda i: (0, 0))
    return pl.pallas_call(
        _out_proj_kernel,
        grid=(t // tm,),
        in_specs=[row(D_MODEL), row(D_SSD), row(D_ATT), full(wo), full(g), full(wr)],
        out_specs=[row(D_MODEL), row(D_MODEL), pl.BlockSpec((N_EXPERTS, tm), lambda i: (0, i))],
        out_shape=[jax.ShapeDtypeStruct((t, D_MODEL), F32),
                   jax.ShapeDtypeStruct((t, D_MODEL), BF16),
                   jax.ShapeDtypeStruct((N_EXPERTS, t), F32)],
        compiler_params=_cparams(("parallel",)),
    )(x2d, ys, ya, wo, g, wr)


def _route_kernel(aff_ref, sel_ref, offs_ref, cnt_ref, *, cap, nb):
    aff = aff_ref[...]
    t = aff.shape[1]
    keys = lax.bitcast_convert_type(aff, I32)
    capf = jnp.float32(cap)

    def count(mask):
        return jnp.sum(jnp.where(mask, 1.0, 0.0), axis=1, keepdims=True)

    def key_step(i, cur):
        cand = cur | lax.shift_left(jnp.int32(1), 30 - i)
        return jnp.where(count(keys >= cand) >= capf, cand, cur)

    thr = lax.fori_loop(0, 31, key_step, jnp.zeros((N_EXPERTS, 1), I32))
    gt = keys > thr
    eq = keys == thr
    need = capf - count(gt)
    idx = lax.broadcasted_iota(I32, (N_EXPERTS, t), 1)
    nbits = max(t.bit_length(), 1)

    def idx_step(i, cur):
        cand = cur | lax.shift_left(jnp.int32(1), nbits - 1 - i)
        return jnp.where(count(jnp.logical_and(eq, idx < cand)) < need, cand, cur)

    last = lax.fori_loop(0, nbits, idx_step, jnp.zeros((N_EXPERTS, 1), I32))
    sel = jnp.where(gt, 1.0, jnp.where(jnp.logical_and(eq, idx <= last), 1.0, 0.0))
    sel_ref[...] = sel

    lane = lax.broadcasted_iota(I32, (N_EXPERTS, LANES), 1)
    cnt = jnp.zeros((N_EXPERTS, LANES), F32)
    for b in range(nb):
        cb = jnp.sum(sel[:, TOK_BLOCK * b:TOK_BLOCK * (b + 1)], axis=1, keepdims=True)
        cnt = jnp.where(lane == b, cb, cnt)
    incl = cnt
    sh = 1
    while sh < LANES:
        incl = incl + jnp.where(lane >= sh, pltpu.roll(incl, sh, axis=1), 0.0)
        sh *= 2
    offs_ref[...] = (incl - cnt).astype(I32)
    cnt_ref[...] = cnt.astype(I32)


def _route(aff_t):
    t = aff_t.shape[1]
    cap = CAP_FACTOR * t // N_EXPERTS
    nb = t // TOK_BLOCK
    assert nb <= LANES
    return pl.pallas_call(
        functools.partial(_route_kernel, cap=cap, nb=nb),
        out_shape=[jax.ShapeDtypeStruct((N_EXPERTS, t), F32),
                   jax.ShapeDtypeStruct((N_EXPERTS, LANES), I32),
                   jax.ShapeDtypeStruct((N_EXPERTS, LANES), I32)],
        compiler_params=pltpu.CompilerParams(vmem_limit_bytes=VMEM_LIMIT),
    )(aff_t)


def _slot_positions(sel):
    si = lax.broadcasted_iota(I32, (TOK_BLOCK, TOK_BLOCK), 0)
    ti = lax.broadcasted_iota(I32, (TOK_BLOCK, TOK_BLOCK), 1)
    tri = jnp.where(si <= ti, 1.0, 0.0).astype(BF16)
    incl = jnp.dot(sel.astype(BF16), tri, preferred_element_type=F32)
    return jnp.where(sel > 0.0, incl - 1.0, -1e6)


def _num_chunks(cnt_ref, b):
    mx = cnt_ref[0, b]
    for e in range(1, N_EXPERTS):
        mx = jnp.maximum(mx, cnt_ref[e, b])
    return (mx + SLOT_STEP - 1) // SLOT_STEP


def _align_down(v):
    return pl.multiple_of((v // SUBLANES) * SUBLANES, SUBLANES)


def _dispatch_kernel(offs_ref, cnt_ref, u_ref, sel_ref, aff_ref, xe_hbm,
                     g_ref, stage_ref, carry_ref, sem, n_ref, *, nb, cap):
    b = pl.program_id(0)

    def copy(slot, e, row):
        return pltpu.make_async_copy(stage_ref.at[slot, e],
                                     xe_hbm.at[e, pl.ds(row, SLOT_WIN), :], sem.at[slot])

    @pl.when(b == 0)
    def _():
        n_ref[0] = 0
        carry_ref[...] = jnp.zeros_like(carry_ref)
        stage_ref[1, 0] = jnp.zeros((SLOT_WIN, 640), U32)
        for e in range(N_EXPERTS):
            pltpu.make_async_copy(stage_ref.at[1, 0], xe_hbm.at[e, pl.ds(cap, SLOT_WIN), :],
                                  sem.at[1]).start()
        for e in range(N_EXPERTS):
            copy(1, 0, 0).wait()

    posm = _slot_positions(sel_ref[...])
    aff = aff_ref[...]
    ub = u_ref[...]
    w_iota = lax.broadcasted_iota(I32, (SLOT_WIN, TOK_BLOCK), 0).astype(F32)
    head_row = lax.broadcasted_iota(I32, (SUBLANES, 640), 0)

    def chunk(k, carry):
        n = n_ref[0]
        slot = lax.rem(n, 2)
        win = []
        for e in range(N_EXPERTS):
            cnt = cnt_ref[e, b]
            done = jnp.minimum(k * SLOT_STEP, cnt)
            base = offs_ref[e, b] + done
            row = _align_down(base)
            end = base + jnp.minimum((k + 1) * SLOT_STEP, cnt) - done
            win.append((row, base - row, _align_down(end) - row))
            ge = posm[e:e + 1, :] == w_iota + (done - (base - row)).astype(F32)
            g_ref[SLOT_WIN * e:SLOT_WIN * (e + 1), :] = jnp.where(ge, 1.0, 0.0).astype(BF16)
            gate = jnp.sum(jnp.where(ge, aff[e:e + 1, :], 0.0), axis=1, keepdims=True)
            stage_ref[slot, e, :, 512:] = _bits(jnp.broadcast_to(gate, (SLOT_WIN, LANES)))
        packed = _pack_halves(jnp.dot(g_ref[...], ub, preferred_element_type=F32))
        for e in range(N_EXPERTS):
            row, skew, nxt = win[e]
            stage_ref[slot, e, :, :512] = packed[SLOT_WIN * e:SLOT_WIN * (e + 1), :]
            stage_ref[slot, e, :SUBLANES, :] = jnp.where(
                head_row < skew, carry_ref[e], stage_ref[slot, e, :SUBLANES, :])
            carry_ref[e] = stage_ref[slot, e, pl.ds(pl.multiple_of(nxt, SUBLANES), SUBLANES), :]

        @pl.when(n > 0)
        def _():
            for e in range(N_EXPERTS):
                copy(1 - slot, e, 0).wait()

        for e in range(N_EXPERTS):
            copy(slot, e, win[e][0]).start()
        n_ref[0] = n + 1
        return carry

    lax.fori_loop(0, _num_chunks(cnt_ref, b), chunk, 0)

    @pl.when(jnp.logical_and(b == nb - 1, n_ref[0] > 0))
    def _():
        slot = lax.rem(n_ref[0] - 1, 2)
        for e in range(N_EXPERTS):
            copy(slot, e, 0).wait()


def _dispatch(u, sel, aff_t, offs, cnt):
    t = u.shape[0]
    cap = CAP_FACTOR * t // N_EXPERTS
    nb = t // TOK_BLOCK
    cap_p = cap + SLOT_WIN
    grid_spec = pltpu.PrefetchScalarGridSpec(
        num_scalar_prefetch=2,
        grid=(nb,),
        in_specs=[pl.BlockSpec((TOK_BLOCK, D_MODEL), lambda b, *_: (b, 0)),
                  pl.BlockSpec((N_EXPERTS, TOK_BLOCK), lambda b, *_: (0, b)),
                  pl.BlockSpec((N_EXPERTS, TOK_BLOCK), lambda b, *_: (0, b))],
        out_specs=pl.BlockSpec(memory_space=pl.ANY),
        scratch_shapes=[pltpu.VMEM((N_EXPERTS * SLOT_WIN, TOK_BLOCK), BF16),
                        pltpu.VMEM((2, N_EXPERTS, SLOT_WIN, 640), U32),
                        pltpu.VMEM((N_EXPERTS, SUBLANES, 640), U32),
                        pltpu.SemaphoreType.DMA((2,)),
                        pltpu.SMEM((1,), I32)])
    return pl.pallas_call(
        functools.partial(_dispatch_kernel, nb=nb, cap=cap),
        grid_spec=grid_spec,
        out_shape=jax.ShapeDtypeStruct((N_EXPERTS, cap_p, 640), U32),
        compiler_params=_cparams(("arbitrary",)),
    )(offs, cnt, u, sel, aff_t)


def _ffn_kernel(xe_ref, wg_ref, wu_ref, wd_ref, o_ref):
    xp = xe_ref[0]
    x = _unpack_halves(xp[:, :512])
    gate = lax.bitcast_convert_type(xp[:, 512:513], F32)
    gp = jnp.dot(x, wg_ref[0, 0], preferred_element_type=F32)
    up = jnp.dot(x, wu_ref[0, 0], preferred_element_type=F32)
    hid = (_silu(gp) * up).astype(BF16)
    out = jnp.dot(hid, wd_ref[0, 0], preferred_element_type=F32) * gate
    o_ref[0] = _pack_halves(out.astype(BF16).astype(F32))


def _ffn(xe, wg, wu, wd, layer, cap):
    tm = min(FFN_TILE, cap)
    return pl.pallas_call(
        _ffn_kernel,
        grid=(N_EXPERTS, cap // tm),
        in_specs=[pl.BlockSpec((1, tm, 640), lambda e, j: (e, j, 0)),
                  pl.BlockSpec((1, 1, D_MODEL, D_FF), lambda e, j: (layer, e, 0, 0)),
                  pl.BlockSpec((1, 1, D_MODEL, D_FF), lambda e, j: (layer, e, 0, 0)),
                  pl.BlockSpec((1, 1, D_FF, D_MODEL), lambda e, j: (layer, e, 0, 0))],
        out_specs=pl.BlockSpec((1, tm, 512), lambda e, j: (e, j, 0)),
        out_shape=jax.ShapeDtypeStruct((N_EXPERTS, cap, 512), U32),
        compiler_params=_cparams(("parallel", "parallel")),
    )(xe, wg, wu, wd)


def _combine_kernel(offs_ref, cnt_ref, x1_ref, sel_ref, gfin_ref, oe_hbm, o_ref,
                    g_ref, stage_ref, acc_ref, sem, *, nb, cap, final_norm):
    b = pl.program_id(0)
    w_iota = lax.broadcasted_iota(I32, (SLOT_WIN, TOK_BLOCK), 0).astype(F32)

    def window(bb, k, e):
        done = jnp.minimum(k * SLOT_STEP, cnt_ref[e, bb])
        base = offs_ref[e, bb] + done
        row = jnp.minimum(_align_down(base), cap - SLOT_WIN)
        return row, done - (base - row)

    def copy(slot, e, row):
        return pltpu.make_async_copy(oe_hbm.at[e, pl.ds(row, SLOT_WIN), :],
                                     stage_ref.at[slot, pl.ds(SLOT_WIN * e, SLOT_WIN), :],
                                     sem.at[slot])

    def fetch(bb, k, slot):
        for e in range(N_EXPERTS):
            copy(slot, e, pl.multiple_of(window(bb, k, e)[0], SUBLANES)).start()

    def wait(slot):
        for e in range(N_EXPERTS):
            copy(slot, e, 0).wait()

    @pl.when(b == 0)
    def _():
        fetch(0, 0, 0)

    @pl.when(b + 1 < nb)
    def _():
        fetch(b + 1, 0, lax.rem(b + 1, 2))

    pos_all = _slot_positions(sel_ref[...])

    def expand(slot, k):
        lo = (k * SLOT_STEP).astype(F32)
        posm = jnp.where(jnp.logical_and(pos_all >= lo, pos_all < lo + SLOT_STEP), pos_all, -1e6)
        for e in range(N_EXPERTS):
            wi = w_iota + window(b, k, e)[1].astype(F32)
            g_ref[SLOT_WIN * e:SLOT_WIN * (e + 1), :] = jnp.where(posm[e:e + 1, :] == wi, 1.0, 0.0)
        gt = g_ref[...].T.astype(BF16)
        slab = _unpack_halves(stage_ref[slot])
        return jnp.dot(gt, slab, preferred_element_type=F32)

    wait(lax.rem(b, 2))
    acc_ref[...] = x1_ref[...] + expand(lax.rem(b, 2), jnp.int32(0))

    def extra(k, carry):
        fetch(b, k, 2)
        wait(2)
        acc_ref[...] += expand(2, k)
        return carry

    lax.fori_loop(1, _num_chunks(cnt_ref, b), extra, 0)
    y = acc_ref[...]
    o_ref[...] = _rms(y, gfin_ref[...]) if final_norm else y


def _combine(x1, sel, oe, offs, cnt, gfin, final_norm):
    t = x1.shape[0]
    cap = CAP_FACTOR * t // N_EXPERTS
    nb = t // TOK_BLOCK
    grid_spec = pltpu.PrefetchScalarGridSpec(
        num_scalar_prefetch=2,
        grid=(nb,),
        in_specs=[pl.BlockSpec((TOK_BLOCK, D_MODEL), lambda b, *_: (b, 0)),
                  pl.BlockSpec((N_EXPERTS, TOK_BLOCK), lambda b, *_: (0, b)),
                  pl.BlockSpec((1, D_MODEL), lambda b, *_: (0, 0)),
                  pl.BlockSpec(memory_space=pl.ANY)],
        out_specs=pl.BlockSpec((TOK_BLOCK, D_MODEL), lambda b, *_: (b, 0)),
        scratch_shapes=[pltpu.VMEM((N_EXPERTS * SLOT_WIN, TOK_BLOCK), F32),
                        pltpu.VMEM((3, N_EXPERTS * SLOT_WIN, 512), U32),
                        pltpu.VMEM((TOK_BLOCK, D_MODEL), F32),
                        pltpu.SemaphoreType.DMA((3,))])
    return pl.pallas_call(
        functools.partial(_combine_kernel, nb=nb, cap=cap, final_norm=final_norm),
        grid_spec=grid_spec,
        out_shape=jax.ShapeDtypeStruct((t, D_MODEL), F32),
        compiler_params=_cparams(("arbitrary",)),
    )(offs, cnt, x1, sel, gfin, oe)


def _prep_layer(w_in, conv_w, conv_b, dt_bias, a_log, d_skip, rpb, w_out, w_router):
    o1 = D_SSD
    o2 = o1 + CONV_DIM
    o3 = o2 + 2 * N_HEADS
    w = jnp.concatenate(
        [w_in[:, :o2], w_in[:, o3:], w_in[:, o2:o3],
         jnp.zeros((D_MODEL, LANES - 2 * N_HEADS), w_in.dtype)], axis=1).astype(BF16)
    lane_pad = lambda v: jnp.concatenate([v.reshape(-1).astype(F32),
                                          jnp.zeros((LANES - 2 * N_HEADS,), F32)])[None, :]
    return dict(
        w_in=w,
        conv_w=jnp.concatenate([conv_w.astype(F32), jnp.zeros((8 - CONV_W, CONV_DIM), F32)], 0),
        conv_b=conv_b.astype(F32)[None, :],
        dt_bias=lane_pad(dt_bias),
        a_lane=lane_pad(-jnp.exp(a_log.astype(F32))),
        d_skip=jnp.repeat(d_skip.astype(F32), 64)[None, :],
        t2=_bias_tables(rpb),
        w_out=w_out.astype(BF16),
        w_router=jnp.concatenate(
            [w_router, jnp.zeros((D_MODEL, LANES - N_EXPERTS), w_router.dtype)], 1).astype(BF16))


def _layer(x2d, b, l, p, experts, layer, g_mix, g_ssd, g_att, g_ffn, g_final, final_norm):
    t = b * l
    row = lambda v: v.astype(F32)[None, :]
    z, xbc, q, k, v, dt = _in_proj(x2d, row(g_mix), p["w_in"])
    r3 = lambda a: a.reshape(b, l, a.shape[-1])
    y_ssd = _ssd(r3(xbc), r3(dt), r3(z), p["conv_w"], p["conv_b"], p["dt_bias"], p["a_lane"],
                 p["d_skip"], row(g_ssd))
    y_att = _attention(r3(q), r3(k), r3(v), p["t2"], row(g_att))
    x1, u, aff_t = _out_proj(x2d, y_ssd.reshape(t, D_SSD), y_att.reshape(t, D_ATT),
                             p["w_out"], row(g_ffn), p["w_router"])
    sel, offs, cnt = _route(aff_t)
    xe = _dispatch(u, sel, aff_t, offs, cnt)
    oe = _ffn(xe, *experts, layer, CAP_FACTOR * t // N_EXPERTS)
    return _combine(x1, sel, oe, offs, cnt, row(g_final), final_norm)


def _trunk(x, layers, experts, norm_mix_g, ssd_norm_g, attn_norm_g, norm_ffn_g, norm_final_g):
    b, l, _ = x.shape
    x2d = x.reshape(b * l, D_MODEL)
    depth = len(layers)
    for i, p in enumerate(layers):
        x2d = _layer(x2d, b, l, p, experts, i, norm_mix_g[i], ssd_norm_g[i], attn_norm_g[i],
                     norm_ffn_g[i], norm_final_g, i == depth - 1)
    return x2d.reshape(b, l, D_MODEL)


def kernel(x_prompt, x_sample, norm_mix_g, w_in, conv_w, conv_b, dt_bias, a_log, d_skip, ssd_norm_g, attn_norm_g, rpb, w_out, norm_ffn_g, w_router, w_gate, w_up, w_down, norm_final_g):
    depth = w_in.shape[0]
    layers = [_prep_layer(w_in[i], conv_w[i], conv_b[i], dt_bias[i], a_log[i], d_skip[i], rpb[i],
                          w_out[i], w_router[i])
              for i in range(depth)]
    experts = (w_gate.astype(BF16), w_up.astype(BF16), w_down.astype(BF16))
    run = functools.partial(_trunk, layers=layers, experts=experts,
                            norm_mix_g=norm_mix_g, ssd_norm_g=ssd_norm_g,
                            attn_norm_g=attn_norm_g, norm_ffn_g=norm_ffn_g,
                            norm_final_g=norm_final_g)
    return (run(x_prompt), run(x_sample))
```

```python
import functools

import jax
import jax.numpy as jnp
from jax import lax
from jax.experimental import pallas as pl
from jax.experimental.pallas import tpu as pltpu

F32 = jnp.float32
BF16 = jnp.bfloat16
U32 = jnp.uint32
I32 = jnp.int32

D_MODEL = 1024
D_SSD = 512
D_ATT = 512
N_HEADS = 8
CONV_W = 5
CONV_DIM = 768
CHUNK = 128
GRID_W = 64
KH = 8
KW = 16
N_EXPERTS = 16
CAP_FACTOR = 2
D_FF = 2048
RMS_EPS = 1e-6
NEG = -1e30
LOG2E = 1.4426950408889634

LANES = 128
HALO = 16
IN_COLS_PADDED = 2944

ROW_TILE = 1024
SSD_SEQS = 1
ATT_ROWS = 8
ATT_KROWS = 16
ATT_VISIT = 5
TOK_BLOCK = 512
SLOT_WIN = 128
SLOT_STEP = 120
SUBLANES = 8
FFN_TILE = 512
VMEM_LIMIT = 56 * 1024 * 1024


def _cparams(sem):
    return pltpu.CompilerParams(dimension_semantics=sem, vmem_limit_bytes=VMEM_LIMIT)


def _sigmoid(x):
    return 1.0 / (1.0 + jnp.exp(-x))


def _silu(x):
    return x * _sigmoid(x)


def _softplus(x):
    return jnp.maximum(x, 0.0) + jnp.log(1.0 + jnp.exp(-jnp.abs(x)))


def _rms(x, g):
    return x * lax.rsqrt(jnp.mean(x * x, axis=-1, keepdims=True) + RMS_EPS) * g


def _bits(x):
    return lax.bitcast_convert_type(x, U32)


def _pack_halves(x):
    lo = _bits(x[:, :512]) >> 16
    hi = _bits(x[:, 512:]) & jnp.uint32(0xFFFF0000)
    return hi | lo


def _unpack_halves(p):
    lo = lax.bitcast_convert_type(p << 16, F32)
    hi = lax.bitcast_convert_type(p & jnp.uint32(0xFFFF0000), F32)
    return jnp.concatenate([lo, hi], axis=1).astype(BF16)


def _in_proj_kernel(x_ref, g_ref, w_ref, z_ref, xbc_ref, q_ref, k_ref, v_ref, dt_ref):
    u = _rms(x_ref[...], g_ref[...]).astype(BF16)

    def proj(a, b):
        return jnp.dot(u, w_ref[:, a:b], preferred_element_type=F32)

    z_ref[...] = proj(0, 512).astype(BF16)
    xbc_ref[...] = proj(512, 1280).astype(BF16)
    q_ref[...] = (proj(1280, 1792) * (64 ** -0.5 * LOG2E)).astype(BF16)
    k_ref[...] = proj(1792, 2304).astype(BF16)
    v_ref[...] = proj(2304, 2816).astype(BF16)
    dt_ref[...] = proj(2816, 2944)


def _in_proj(x2d, g, w):
    t = x2d.shape[0]
    tm = min(ROW_TILE, t)
    row = lambda n: pl.BlockSpec((tm, n), lambda i: (i, 0))
    full = lambda a: pl.BlockSpec(a.shape, lambda i: (0, 0))
    return pl.pallas_call(
        _in_proj_kernel,
        grid=(t // tm,),
        in_specs=[row(D_MODEL), full(g), full(w)],
        out_specs=[row(512), row(CONV_DIM), row(512), row(512), row(512), row(LANES)],
        out_shape=[jax.ShapeDtypeStruct((t, 512), BF16),
                   jax.ShapeDtypeStruct((t, CONV_DIM), BF16),
                   jax.ShapeDtypeStruct((t, 512), BF16),
                   jax.ShapeDtypeStruct((t, 512), BF16),
                   jax.ShapeDtypeStruct((t, 512), BF16),
                   jax.ShapeDtypeStruct((t, LANES), F32)],
        compiler_params=_cparams(("parallel",)),
    )(x2d, g, w)


def _conv_silu(xc_ref, xp_ref, xn_ref, cw_ref, cb_ref, c, nc, s):
    cur = xc_ref[s].astype(F32)
    prev = jnp.where(c > 0, xp_ref[s].astype(F32), 0.0)
    nxt = jnp.where(c < nc - 1, xn_ref[s].astype(F32), 0.0)
    xpad = jnp.concatenate([prev, cur, nxt], axis=0)
    acc = jnp.broadcast_to(cb_ref[...], (CHUNK, CONV_DIM))
    for k in range(CONV_W):
        start = HALO - CONV_W // 2 + k
        acc = acc + cw_ref[k:k + 1, :] * xpad[start:start + CHUNK, :]
    return _silu(acc)


def _decay_terms(dt_ref, dtb_ref, a_ref, s):
    dtv = _softplus(dt_ref[s] + dtb_ref[...])
    a = dtv * a_ref[...]
    li = lax.broadcasted_iota(I32, (CHUNK, CHUNK), 0)
    si = lax.broadcasted_iota(I32, (CHUNK, CHUNK), 1)
    tri = jnp.where(si <= li, 1.0, 0.0).astype(F32)
    cs = jnp.dot(tri, a, preferred_element_type=F32, precision=lax.Precision.HIGHEST)
    tot = cs[CHUNK - 1:CHUNK, :]
    return dtv, cs, cs - a, tot


def _pair_cols(x, h0, left):
    return jnp.where(left, x[:, h0:h0 + 1], x[:, h0 + 1:h0 + 2])


def _ssd_bwd_state_kernel(xc_ref, xp_ref, xn_ref, dt_ref, cw_ref, cb_ref, dtb_ref, a_ref,
                          sin_ref, act_ref, sb_ref, *, nc):
    i = pl.program_id(1)
    c = nc - 1 - i

    @pl.when(i == 0)
    def _():
        sb_ref[...] = jnp.zeros_like(sb_ref)

    left = lax.broadcasted_iota(I32, (1, LANES), 1) < 64
    for s in range(SSD_SEQS):
        sin_ref[s, 0] = sb_ref[s]
        act = _conv_silu(xc_ref, xp_ref, xn_ref, cw_ref, cb_ref, c, nc, s)
        act_ref[s] = act.astype(BF16)
        xs = act[:, :D_SSD]
        bt = act[:, D_SSD:D_SSD + LANES].T.astype(BF16)
        dtv, _, ex, tot = _decay_terms(dt_ref, dtb_ref, a_ref, s)
        wb = jnp.exp(ex) * dtv
        decb = jnp.exp(tot)
        for j in range(4):
            xw = (xs[:, LANES * j:LANES * (j + 1)] * _pair_cols(wb, 8 + 2 * j, left)).astype(BF16)
            st = jnp.dot(bt, xw, preferred_element_type=F32)
            sb_ref[s, j] = _pair_cols(decb, 8 + 2 * j, left) * sb_ref[s, j] + st


def _ssd_main_kernel(act_ref, dt_ref, z_ref, sin_ref, sel_ref, dtb_ref, a_ref, dskip_ref, g_ref,
                     o_ref, sf_ref, y_ref):
    c = pl.program_id(1)

    @pl.when(c == 0)
    def _():
        sf_ref[...] = jnp.zeros_like(sf_ref)

    lane = lax.broadcasted_iota(I32, (1, LANES), 1)
    left = lane < 64
    li = lax.broadcasted_iota(I32, (CHUNK, CHUNK), 0)
    si = lax.broadcasted_iota(I32, (CHUNK, CHUNK), 1)
    for s in range(SSD_SEQS):
        bm_b = act_ref[s, :, D_SSD:D_SSD + LANES]
        cm_b = act_ref[s, :, D_SSD + LANES:]
        dtv, cs, ex, tot = _decay_terms(dt_ref, dtb_ref, a_ref, s)
        cs_t, ex_t, dt_t = cs.T, ex.T, dtv.T
        spread = lambda v, k: jnp.dot(v.astype(BF16), sel_ref[k], preferred_element_type=F32)
        wf = spread(jnp.exp(tot - cs) * dtv, 0)
        ef = spread(jnp.exp(cs), 0)
        eb = spread(jnp.exp(tot - ex), 1)
        decf = jnp.exp(tot)
        bt = bm_b.astype(F32).T.astype(BF16)
        zero = jnp.zeros_like(bm_b)
        for grp in range(2):
            in_grp = left if grp == 0 else jnp.logical_not(left)
            b_g = jnp.where(in_grp, bm_b, zero)
            c_g = jnp.where(in_grp, cm_b, zero)
            cb_g = lax.dot_general(cm_b, b_g, (((1,), (1,)), ((), ())),
                                   preferred_element_type=F32)
            for jj in range(2):
                j = 2 * grp + jj
                cols = slice(LANES * j, LANES * (j + 1))
                x_pb = act_ref[s, :, cols]
                x_p = x_pb.astype(F32)
                halves = []
                for sub in range(2):
                    h = 2 * j + sub
                    dtf = dt_t[h:h + 1, :]
                    dtb = dt_t[8 + h:9 + h, :]
                    arg = jnp.where(si <= li,
                                    cs[:, h:h + 1] - cs_t[h:h + 1, :],
                                    ex_t[8 + h:9 + h, :] - ex[:, 8 + h:9 + h])
                    dsel = jnp.where(si < li, dtf, jnp.where(si > li, dtb, dtf + dtb))
                    m = (cb_g * jnp.exp(arg) * dsel).astype(BF16)
                    halves.append(jnp.dot(m, x_pb, preferred_element_type=F32))
                y = jnp.where(left, halves[0], halves[1])
                sf = sf_ref[s, j]
                sb = sin_ref[s, 0, j]
                y = y + jnp.dot(c_g, sf.astype(BF16), preferred_element_type=F32) * ef[:, cols]
                y = y + jnp.dot(c_g, sb.astype(BF16), preferred_element_type=F32) * eb[:, cols]
                y_ref[s, :, cols] = y + dskip_ref[:, cols] * x_p
                xw = (x_p * wf[:, cols]).astype(BF16)
                st = jnp.dot(bt, xw, preferred_element_type=F32)
                sf_ref[s, j] = _pair_cols(decf, 2 * j, left) * sf + st

        yz = y_ref[s] * _silu(z_ref[s].astype(F32))
        o_ref[s] = _rms(yz, g_ref[...]).astype(BF16)


def _ssd(xbc, dt, z, cw, cb, dtb, a_lane, dskip, g):
    b, l, _ = xbc.shape
    nc = l // CHUNK
    hb = CHUNK // HALO
    last_halo = l // HALO - 1

    assert b % SSD_SEQS == 0
    ns = SSD_SEQS
    rev = lambda i: nc - 1 - i
    const = lambda a: pl.BlockSpec(a.shape, lambda bi, i: (0,) * a.ndim)
    state = lambda chunk_of: pl.BlockSpec((ns, 1, 4, LANES, LANES),
                                          lambda bi, i: (bi, chunk_of(i), 0, 0, 0))
    chunk = lambda n, chunk_of: pl.BlockSpec((ns, CHUNK, n), lambda bi, i: (bi, chunk_of(i), 0))
    prev = pl.BlockSpec((ns, HALO, CONV_DIM),
                        lambda bi, i: (bi, jnp.maximum(rev(i) * hb - 1, 0), 0))
    nxt = pl.BlockSpec((ns, HALO, CONV_DIM),
                       lambda bi, i: (bi, jnp.minimum((rev(i) + 1) * hb, last_halo), 0))
    s_in, act = pl.pallas_call(
        functools.partial(_ssd_bwd_state_kernel, nc=nc),
        grid=(b // ns, nc),
        in_specs=[chunk(CONV_DIM, rev), prev, nxt, chunk(LANES, rev),
                  const(cw), const(cb), const(dtb), const(a_lane)],
        out_specs=[state(rev), chunk(CONV_DIM, rev)],
        out_shape=[jax.ShapeDtypeStruct((b, nc, 4, LANES, LANES), F32),
                   jax.ShapeDtypeStruct((b, l, CONV_DIM), BF16)],
        scratch_shapes=[pltpu.VMEM((ns, 4, LANES, LANES), F32)],
        compiler_params=_cparams(("parallel", "arbitrary")),
    )(xbc, xbc, xbc, dt, cw, cb, dtb, a_lane)

    head_of_lane = jnp.arange(D_SSD)[None, :] // 64
    row = jnp.arange(LANES)[:, None]
    sel = jnp.stack([(row == head_of_lane + N_HEADS * k) for k in range(2)]).astype(BF16)
    fwd = lambda i: i
    return pl.pallas_call(
        _ssd_main_kernel,
        grid=(b // ns, nc),
        in_specs=[chunk(CONV_DIM, fwd), chunk(LANES, fwd), chunk(D_SSD, fwd), state(fwd),
                  const(sel), const(dtb), const(a_lane), const(dskip), const(g)],
        out_specs=chunk(D_SSD, fwd),
        out_shape=jax.ShapeDtypeStruct((b, l, D_SSD), BF16),
        scratch_shapes=[pltpu.VMEM((ns, 4, LANES, LANES), F32),
                        pltpu.VMEM((ns, CHUNK, D_SSD), F32)],
        compiler_params=_cparams(("parallel", "arbitrary")),
    )(act, dt, z, s_in, sel, dtb, a_lane, dskip, g)


def _att_kernel(q_ref, k0_ref, k1_ref, k2_ref, k3_ref, v0_ref, v1_ref, v2_ref, v3_ref,
                t2_ref, g_ref, o_ref, s_ref, t_ref, p_ref, linv_ref, ov_ref, acc_ref, *, rows):
    blk = pl.program_id(1)
    lane = lax.broadcasted_iota(I32, (1, LANES), 1)
    left = lane < 64
    n_pairs = ATT_KROWS // 2
    zero_tile = jnp.zeros((GRID_W, LANES), BF16)

    def softmax_tail(par, rq, mx, kcols_of, n_tiles):
        m = jnp.max(mx, axis=1, keepdims=True)
        lsum = jnp.zeros((GRID_W, LANES), F32)
        for d in range(n_tiles):
            p = jnp.exp2(t_ref[par, rq, LANES * d:LANES * (d + 1)] - m)
            lsum = lsum + p
            p_ref[par, rq, kcols_of(d)] = p.astype(BF16)
        tot = jnp.sum(lsum, axis=1, keepdims=True)
        linv_ref[par, rq, :] = jnp.broadcast_to(1.0 / tot, (GRID_W, LANES))

    def window(i, place):
        koff = {"top": 0, "interior": -(KH // 2), "bottom": -KH}[place]
        js = {"top": max(i - KH // 2, 0), "interior": i, "bottom": min(i + KH // 2, KH)}[place]
        return koff, js, js // 2, (js + KH - 1) // 2

    def softmax_row(i, h, par, place):
        koff, js, first, last = window(i, place)
        rq = slice(GRID_W * i, GRID_W * (i + 1))
        kcols_of = lambda d: slice(LANES * (first + d), LANES * (first + d + 1))
        mx = None
        for d in range(last - first + 1):
            jl = 2 * (first + d)
            t = s_ref[par, rq, kcols_of(d)] + t2_ref[h, koff + jl - i + KH]
            if jl < js:
                t = jnp.where(left, NEG, t)
            if jl + 1 > js + KH - 1:
                t = jnp.where(left, t, NEG)
            t_ref[par, rq, LANES * d:LANES * (d + 1)] = t
            mx = t if mx is None else jnp.maximum(mx, t)
        softmax_tail(par, rq, mx, kcols_of, last - first + 1)
        for jp in range(2 * (first // 2), 2 * (last // 2) + 2):
            if jp < first or jp > last:
                p_ref[par, rq, LANES * jp:LANES * (jp + 1)] = zero_tile

    def all_heads(place):
        kt_rows = []
        for kt in range(n_pairs // 2):
            vis = [i for i in range(ATT_ROWS)
                   if window(i, place)[2] // 2 <= kt <= window(i, place)[3] // 2]
            if vis:
                kt_rows.append((kt, vis[0], vis[-1] + 1))
        k_refs = (k0_ref, k1_ref, k2_ref, k3_ref)
        v_refs = (v0_ref, v1_ref, v2_ref, v3_ref)
        for hp in range(N_HEADS // 2):
            cols = slice(LANES * hp, LANES * (hp + 1))
            o_pair = None
            for sub in range(2):
                h = 2 * hp + sub
                in_head = left if sub == 0 else jnp.logical_not(left)
                for kt, ra, rb in kt_rows:
                    keys = slice(2 * LANES * kt, 2 * LANES * (kt + 1))
                    k_tile = k_refs[kt][0, :, cols]
                    s_ref[sub, GRID_W * ra:GRID_W * rb, keys] = lax.dot_general(
                        q_ref[0, GRID_W * ra:GRID_W * rb, cols],
                        jnp.where(in_head, k_tile, jnp.zeros_like(k_tile)),
                        (((1,), (1,)), ((), ())), preferred_element_type=F32)
                for i in range(ATT_ROWS):
                    softmax_row(i, h, sub, place)
                done = 0
                for kt, ra, rb in kt_rows:
                    keys = slice(2 * LANES * kt, 2 * LANES * (kt + 1))
                    part = jnp.dot(p_ref[sub, GRID_W * ra:GRID_W * rb, keys],
                                   v_refs[kt][0, :, cols], preferred_element_type=F32)
                    old = max(min(rb, done) - ra, 0)
                    if old:
                        ov_ref[sub, GRID_W * ra:GRID_W * (ra + old), :] += part[:GRID_W * old]
                    if ra + old < rb:
                        ov_ref[sub, GRID_W * (ra + old):GRID_W * rb, :] = part[GRID_W * old:]
                    done = max(done, rb)
                o_full = ov_ref[sub] * linv_ref[sub]
                o_pair = o_full if sub == 0 else jnp.where(left, o_pair, o_full)
            acc_ref[:, cols] = o_pair

    last_blk = rows // ATT_ROWS - 1
    pl.when(blk == 0)(lambda: all_heads("top"))
    pl.when(jnp.logical_and(blk > 0, blk < last_blk))(lambda: all_heads("interior"))
    pl.when(blk == last_blk)(lambda: all_heads("bottom"))
    o_ref[0] = _rms(acc_ref[...], g_ref[...]).astype(BF16)


def _attention(q, k, v, t2, g):
    b, l, _ = q.shape
    rows = l // GRID_W
    assert rows >= ATT_KROWS and rows % ATT_ROWS == 0
    nblk = rows // ATT_ROWS
    nq = ATT_ROWS * GRID_W
    kb = 4 * GRID_W
    n_kb = l // kb

    def kspec(m):
        return pl.BlockSpec(
            (1, kb, D_ATT),
            lambda bi, i, m=m: (bi, jnp.clip(2 * i - 1, 0, n_kb - 4) + m, 0))

    const = lambda a: pl.BlockSpec(a.shape, lambda bi, i: (0,) * a.ndim)
    return pl.pallas_call(
        functools.partial(_att_kernel, rows=rows),
        grid=(b, nblk),
        in_specs=[pl.BlockSpec((1, nq, D_ATT), lambda bi, i: (bi, i, 0))]
                 + [kspec(m) for m in range(4)] + [kspec(m) for m in range(4)]
                 + [const(t2), const(g)],
        out_specs=pl.BlockSpec((1, nq, D_ATT), lambda bi, i: (bi, i, 0)),
        out_shape=jax.ShapeDtypeStruct((b, l, D_ATT), BF16),
        scratch_shapes=[pltpu.VMEM((2, nq, ATT_KROWS * GRID_W), F32),
                        pltpu.VMEM((2, nq, ATT_VISIT * LANES), F32),
                        pltpu.VMEM((2, nq, ATT_KROWS * GRID_W), BF16),
                        pltpu.VMEM((2, nq, LANES), F32),
                        pltpu.VMEM((2, nq, LANES), F32),
                        pltpu.VMEM((nq, D_ATT), F32)],
        compiler_params=_cparams(("parallel", "parallel")),
    )(q, k, k, k, k, v, v, v, v, t2, g)


def _bias_tables(rpb):
    c = jnp.arange(GRID_W)
    col_start = jnp.clip(c - KW // 2, 0, GRID_W - KW)
    col_mask = (c[None, :] >= col_start[:, None]) & (c[None, :] < col_start[:, None] + KW)
    dc = jnp.clip(c[None, :] - c[:, None] + (KW - 1), 0, 2 * KW - 2)
    t = rpb.astype(F32)[:, :, dc]
    t = jnp.where(col_mask[None, None], t * LOG2E, NEG)
    pad = jnp.full((N_HEADS, 1, GRID_W, GRID_W), NEG, F32)
    t = jnp.concatenate([pad, t, pad], axis=1)
    return jnp.concatenate([t[:, :-1], t[:, 1:]], axis=-1)


def _out_proj_kernel(x_ref, ys_ref, ya_ref, wo_ref, g_ref, wr_ref, x1_ref, u_ref, aff_ref):
    x1 = x_ref[...] \
        + jnp.dot(ys_ref[...], wo_ref[:D_SSD, :], preferred_element_type=F32) \
        + jnp.dot(ya_ref[...], wo_ref[D_SSD:, :], preferred_element_type=F32)
    x1_ref[...] = x1
    ub = _rms(x1, g_ref[...]).astype(BF16)
    u_ref[...] = ub
    logits = jnp.dot(ub, wr_ref[...], preferred_element_type=F32)
    lt = logits.T[:N_EXPERTS, :]
    e = jnp.exp(lt - jnp.max(lt, axis=0, keepdims=True))
    aff_ref[...] = e / jnp.sum(e, axis=0, keepdims=True)


def _out_proj(x2d, ys, ya, wo, g, wr):
    t = x2d.shape[0]
    tm = min(ROW_TILE, t)
    row = lambda n: pl.BlockSpec((tm, n), lambda i: (i, 0))
    full = lambda a: pl.BlockSpec(a.shape, lambda i: (0, 0))
    return pl.pallas_call(
        _out_proj_kernel,
        grid=(t // tm,),
        in_specs=[row(D_MODEL), row(D_SSD), row(D_ATT), full(wo), full(g), full(wr)],
        out_specs=[row(D_MODEL), row(D_MODEL), pl.BlockSpec((N_EXPERTS, tm), lambda i: (0, i))],
        out_shape=[jax.ShapeDtypeStruct((t, D_MODEL), F32),
                   jax.ShapeDtypeStruct((t, D_MODEL), BF16),
                   jax.ShapeDtypeStruct((N_EXPERTS, t), F32)],
        compiler_params=_cparams(("parallel",)),
    )(x2d, ys, ya, wo, g, wr)


def _route_kernel(aff_ref, sel_ref, offs_ref, cnt_ref, *, cap, nb):
    aff = aff_ref[...]
    t = aff.shape[1]
    keys = lax.bitcast_convert_type(aff, I32)
    capf = jnp.float32(cap)

    def count(mask):
        return jnp.sum(jnp.where(mask, 1.0, 0.0), axis=1, keepdims=True)

    def key_step(i, cur):
        cand = cur | lax.shift_left(jnp.int32(1), 30 - i)
        return jnp.where(count(keys >= cand) >= capf, cand, cur)

    thr = lax.fori_loop(0, 31, key_step, jnp.zeros((N_EXPERTS, 1), I32))
    gt = keys > thr
    eq = keys == thr
    need = capf - count(gt)
    idx = lax.broadcasted_iota(I32, (N_EXPERTS, t), 1)
    nbits = max(t.bit_length(), 1)

    def idx_step(i, cur):
        cand = cur | lax.shift_left(jnp.int32(1), nbits - 1 - i)
        return jnp.where(count(jnp.logical_and(eq, idx < cand)) < need, cand, cur)

    last = lax.fori_loop(0, nbits, idx_step, jnp.zeros((N_EXPERTS, 1), I32))
    sel = jnp.where(gt, 1.0, jnp.where(jnp.logical_and(eq, idx <= last), 1.0, 0.0))
    sel_ref[...] = sel

    lane = lax.broadcasted_iota(I32, (N_EXPERTS, LANES), 1)
    cnt = jnp.zeros((N_EXPERTS, LANES), F32)
    for b in range(nb):
        cb = jnp.sum(sel[:, TOK_BLOCK * b:TOK_BLOCK * (b + 1)], axis=1, keepdims=True)
        cnt = jnp.where(lane == b, cb, cnt)
    incl = cnt
    sh = 1
    while sh < LANES:
        incl = incl + jnp.where(lane >= sh, pltpu.roll(incl, sh, axis=1), 0.0)
        sh *= 2
    offs_ref[...] = (incl - cnt).astype(I32)
    cnt_ref[...] = cnt.astype(I32)


def _route(aff_t):
    t = aff_t.shape[1]
    cap = CAP_FACTOR * t // N_EXPERTS
    nb = t // TOK_BLOCK
    assert nb <= LANES
    return pl.pallas_call(
        functools.partial(_route_kernel, cap=cap, nb=nb),
        out_shape=[jax.ShapeDtypeStruct((N_EXPERTS, t), F32),
                   jax.ShapeDtypeStruct((N_EXPERTS, LANES), I32),
                   jax.ShapeDtypeStruct((N_EXPERTS, LANES), I32)],
        compiler_params=pltpu.CompilerParams(vmem_limit_bytes=VMEM_LIMIT),
    )(aff_t)


def _slot_positions(sel):
    si = lax.broadcasted_iota(I32, (TOK_BLOCK, TOK_BLOCK), 0)
    ti = lax.broadcasted_iota(I32, (TOK_BLOCK, TOK_BLOCK), 1)
    tri = jnp.where(si <= ti, 1.0, 0.0).astype(BF16)
    incl = jnp.dot(sel.astype(BF16), tri, preferred_element_type=F32)
    return jnp.where(sel > 0.0, incl - 1.0, -1e6)


def _num_chunks(cnt_ref, b):
    mx = cnt_ref[0, b]
    for e in range(1, N_EXPERTS):
        mx = jnp.maximum(mx, cnt_ref[e, b])
    return (mx + SLOT_STEP - 1) // SLOT_STEP


def _align_down(v):
    return pl.multiple_of((v // SUBLANES) * SUBLANES, SUBLANES)


def _dispatch_kernel(offs_ref, cnt_ref, u_ref, sel_ref, aff_ref, xe_hbm,
                     g_ref, stage_ref, carry_ref, sem, n_ref, *, nb, cap):
    b = pl.program_id(0)

    def copy(slot, e, row):
        return pltpu.make_async_copy(stage_ref.at[slot, e],
                                     xe_hbm.at[e, pl.ds(row, SLOT_WIN), :], sem.at[slot])

    @pl.when(b == 0)
    def _():
        n_ref[0] = 0
        carry_ref[...] = jnp.zeros_like(carry_ref)
        stage_ref[1, 0] = jnp.zeros((SLOT_WIN, 640), U32)
        for e in range(N_EXPERTS):
            pltpu.make_async_copy(stage_ref.at[1, 0], xe_hbm.at[e, pl.ds(cap, SLOT_WIN), :],
                                  sem.at[1]).start()
        for e in range(N_EXPERTS):
            copy(1, 0, 0).wait()

    posm = _slot_positions(sel_ref[...])
    aff = aff_ref[...]
    ub = u_ref[...]
    w_iota = lax.broadcasted_iota(I32, (SLOT_WIN, TOK_BLOCK), 0).astype(F32)
    head_row = lax.broadcasted_iota(I32, (SUBLANES, 640), 0)

    def chunk(k, carry):
        n = n_ref[0]
        slot = lax.rem(n, 2)
        win = []
        for e in range(N_EXPERTS):
            cnt = cnt_ref[e, b]
            done = jnp.minimum(k * SLOT_STEP, cnt)
            base = offs_ref[e, b] + done
            row = _align_down(base)
            end = base + jnp.minimum((k + 1) * SLOT_STEP, cnt) - done
            win.append((row, base - row, _align_down(end) - row))
            ge = posm[e:e + 1, :] == w_iota + (done - (base - row)).astype(F32)
            g_ref[SLOT_WIN * e:SLOT_WIN * (e + 1), :] = jnp.where(ge, 1.0, 0.0).astype(BF16)
            gate = jnp.sum(jnp.where(ge, aff[e:e + 1, :], 0.0), axis=1, keepdims=True)
            stage_ref[slot, e, :, 512:] = _bits(jnp.broadcast_to(gate, (SLOT_WIN, LANES)))
        packed = _pack_halves(jnp.dot(g_ref[...], ub, preferred_element_type=F32))
        for e in range(N_EXPERTS):
            row, skew, nxt = win[e]
            stage_ref[slot, e, :, :512] = packed[SLOT_WIN * e:SLOT_WIN * (e + 1), :]
            stage_ref[slot, e, :SUBLANES, :] = jnp.where(
                head_row < skew, carry_ref[e], stage_ref[slot, e, :SUBLANES, :])
            carry_ref[e] = stage_ref[slot, e, pl.ds(pl.multiple_of(nxt, SUBLANES), SUBLANES), :]

        @pl.when(n > 0)
        def _():
            for e in range(N_EXPERTS):
                copy(1 - slot, e, 0).wait()

        for e in range(N_EXPERTS):
            copy(slot, e, win[e][0]).start()
        n_ref[0] = n + 1
        return carry

    lax.fori_loop(0, _num_chunks(cnt_ref, b), chunk, 0)

    @pl.when(jnp.logical_and(b == nb - 1, n_ref[0] > 0))
    def _():
        slot = lax.rem(n_ref[0] - 1, 2)
        for e in range(N_EXPERTS):
            copy(slot, e, 0).wait()


def _dispatch(u, sel, aff_t, offs, cnt):
    t = u.shape[0]
    cap = CAP_FACTOR * t // N_EXPERTS
    nb = t // TOK_BLOCK
    cap_p = cap + SLOT_WIN
    grid_spec = pltpu.PrefetchScalarGridSpec(
        num_scalar_prefetch=2,
        grid=(nb,),
        in_specs=[pl.BlockSpec((TOK_BLOCK, D_MODEL), lambda b, *_: (b, 0)),
                  pl.BlockSpec((N_EXPERTS, TOK_BLOCK), lambda b, *_: (0, b)),
                  pl.BlockSpec((N_EXPERTS, TOK_BLOCK), lambda b, *_: (0, b))],
        out_specs=pl.BlockSpec(memory_space=pl.ANY),
        scratch_shapes=[pltpu.VMEM((N_EXPERTS * SLOT_WIN, TOK_BLOCK), BF16),
                        pltpu.VMEM((2, N_EXPERTS, SLOT_WIN, 640), U32),
                        pltpu.VMEM((N_EXPERTS, SUBLANES, 640), U32),
                        pltpu.SemaphoreType.DMA((2,)),
                        pltpu.SMEM((1,), I32)])
    return pl.pallas_call(
        functools.partial(_dispatch_kernel, nb=nb, cap=cap),
        grid_spec=grid_spec,
        out_shape=jax.ShapeDtypeStruct((N_EXPERTS, cap_p, 640), U32),
        compiler_params=_cparams(("arbitrary",)),
    )(offs, cnt, u, sel, aff_t)


def _ffn_kernel(xe_ref, wg_ref, wu_ref, wd_ref, o_ref):
    xp = xe_ref[0]
    x = _unpack_halves(xp[:, :512])
    gate = lax.bitcast_convert_type(xp[:, 512:513], F32)
    gp = jnp.dot(x, wg_ref[0, 0], preferred_element_type=F32)
    up = jnp.dot(x, wu_ref[0, 0], preferred_element_type=F32)
    hid = (_silu(gp) * up).astype(BF16)
    out = jnp.dot(hid, wd_ref[0, 0], preferred_element_type=F32) * gate
    o_ref[0] = _pack_halves(out.astype(BF16).astype(F32))


def _ffn(xe, wg, wu, wd, layer, cap):
    tm = min(FFN_TILE, cap)
    return pl.pallas_call(
        _ffn_kernel,
        grid=(N_EXPERTS, cap // tm),
        in_specs=[pl.BlockSpec((1, tm, 640), lambda e, j: (e, j, 0)),
                  pl.BlockSpec((1, 1, D_MODEL, D_FF), lambda e, j: (layer, e, 0, 0)),
                  pl.BlockSpec((1, 1, D_MODEL, D_FF), lambda e, j: (layer, e, 0, 0)),
                  pl.BlockSpec((1, 1, D_FF, D_MODEL), lambda e, j: (layer, e, 0, 0))],
        out_specs=pl.BlockSpec((1, tm, 512), lambda e, j: (e, j, 0)),
        out_shape=jax.ShapeDtypeStruct((N_EXPERTS, cap, 512), U32),
        compiler_params=_cparams(("parallel", "parallel")),
    )(xe, wg, wu, wd)


def _combine_kernel(offs_ref, cnt_ref, x1_ref, sel_ref, gfin_ref, oe_hbm, o_ref,
                    g_ref, stage_ref, acc_ref, sem, *, nb, cap, final_norm):
    b = pl.program_id(0)
    w_iota = lax.broadcasted_iota(I32, (SLOT_WIN, TOK_BLOCK), 0).astype(F32)

    def window(bb, k, e):
        done = jnp.minimum(k * SLOT_STEP, cnt_ref[e, bb])
        base = offs_ref[e, bb] + done
        row = jnp.minimum(_align_down(base), cap - SLOT_WIN)
        return row, done - (base - row)

    def copy(slot, e, row):
        return pltpu.make_async_copy(oe_hbm.at[e, pl.ds(row, SLOT_WIN), :],
                                     stage_ref.at[slot, pl.ds(SLOT_WIN * e, SLOT_WIN), :],
                                     sem.at[slot])

    def fetch(bb, k, slot):
        for e in range(N_EXPERTS):
            copy(slot, e, pl.multiple_of(window(bb, k, e)[0], SUBLANES)).start()

    def wait(slot):
        for e in range(N_EXPERTS):
            copy(slot, e, 0).wait()

    @pl.when(b == 0)
    def _():
        fetch(0, 0, 0)

    @pl.when(b + 1 < nb)
    def _():
        fetch(b + 1, 0, lax.rem(b + 1, 2))

    pos_all = _slot_positions(sel_ref[...])

    def expand(slot, k):
        lo = (k * SLOT_STEP).astype(F32)
        posm = jnp.where(jnp.logical_and(pos_all >= lo, pos_all < lo + SLOT_STEP), pos_all, -1e6)
        for e in range(N_EXPERTS):
            wi = w_iota + window(b, k, e)[1].astype(F32)
            g_ref[SLOT_WIN * e:SLOT_WIN * (e + 1), :] = jnp.where(posm[e:e + 1, :] == wi, 1.0, 0.0)
        gt = g_ref[...].T.astype(BF16)
        slab = _unpack_halves(stage_ref[slot])
        return jnp.dot(gt, slab, preferred_element_type=F32)

    wait(lax.rem(b, 2))
    acc_ref[...] = x1_ref[...] + expand(lax.rem(b, 2), jnp.int32(0))

    def extra(k, carry):
        fetch(b, k, 2)
        wait(2)
        acc_ref[...] += expand(2, k)
        return carry

    lax.fori_loop(1, _num_chunks(cnt_ref, b), extra, 0)
    y = acc_ref[...]
    o_ref[...] = _rms(y, gfin_ref[...]) if final_norm else y


def _combine(x1, sel, oe, offs, cnt, gfin, final_norm):
    t = x1.shape[0]
    cap = CAP_FACTOR * t // N_EXPERTS
    nb = t // TOK_BLOCK
    grid_spec = pltpu.PrefetchScalarGridSpec(
        num_scalar_prefetch=2,
        grid=(nb,),
        in_specs=[pl.BlockSpec((TOK_BLOCK, D_MODEL), lambda b, *_: (b, 0)),
                  pl.BlockSpec((N_EXPERTS, TOK_BLOCK), lambda b, *_: (0, b)),
                  pl.BlockSpec((1, D_MODEL), lambda b, *_: (0, 0)),
                  pl.BlockSpec(memory_space=pl.ANY)],
        out_specs=pl.BlockSpec((TOK_BLOCK, D_MODEL), lambda b, *_: (b, 0)),
        scratch_shapes=[pltpu.VMEM((N_EXPERTS * SLOT_WIN, TOK_BLOCK), F32),
                        pltpu.VMEM((3, N_EXPERTS * SLOT_WIN, 512), U32),
                        pltpu.VMEM((TOK_BLOCK, D_MODEL), F32),
                        pltpu.SemaphoreType.DMA((3,))])
    return pl.pallas_call(
        functools.partial(_combine_kernel, nb=nb, cap=cap, final_norm=final_norm),
        grid_spec=grid_spec,
        out_shape=jax.ShapeDtypeStruct((t, D_MODEL), F32),
        compiler_params=_cparams(("arbitrary",)),
    )(offs, cnt, x1, sel, gfin, oe)


def _prep_layer(w_in, conv_w, conv_b, dt_bias, a_log, d_skip, rpb, w_out, w_router):
    o1 = D_SSD
    o2 = o1 + CONV_DIM
    o3 = o2 + 2 * N_HEADS
    w = jnp.concatenate(
        [w_in[:, :o2], w_in[:, o3:], w_in[:, o2:o3],
         jnp.zeros((D_MODEL, LANES - 2 * N_HEADS), w_in.dtype)], axis=1).astype(BF16)
    lane_pad = lambda v: jnp.concatenate([v.reshape(-1).astype(F32),
                                          jnp.zeros((LANES - 2 * N_HEADS,), F32)])[None, :]
    return dict(
        w_in=w,
        conv_w=jnp.concatenate([conv_w.astype(F32), jnp.zeros((8 - CONV_W, CONV_DIM), F32)], 0),
        conv_b=conv_b.astype(F32)[None, :],
        dt_bias=lane_pad(dt_bias),
        a_lane=lane_pad(-jnp.exp(a_log.astype(F32))),
        d_skip=jnp.repeat(d_skip.astype(F32), 64)[None, :],
        t2=_bias_tables(rpb),
        w_out=w_out.astype(BF16),
        w_router=jnp.concatenate(
            [w_router, jnp.zeros((D_MODEL, LANES - N_EXPERTS), w_router.dtype)], 1).astype(BF16))


def _layer(x2d, b, l, p, experts, layer, g_mix, g_ssd, g_att, g_ffn, g_final, final_norm):
    t = b * l
    row = lambda v: v.astype(F32)[None, :]
    z, xbc, q, k, v, dt = _in_proj(x2d, row(g_mix), p["w_in"])
    r3 = lambda a: a.reshape(b, l, a.shape[-1])
    y_ssd = _ssd(r3(xbc), r3(dt), r3(z), p["conv_w"], p["conv_b"], p["dt_bias"], p["a_lane"],
                 p["d_skip"], row(g_ssd))
    y_att = _attention(r3(q), r3(k), r3(v), p["t2"], row(g_att))
    x1, u, aff_t = _out_proj(x2d, y_ssd.reshape(t, D_SSD), y_att.reshape(t, D_ATT),
                             p["w_out"], row(g_ffn), p["w_router"])
    sel, offs, cnt = _route(aff_t)
    xe = _dispatch(u, sel, aff_t, offs, cnt)
    oe = _ffn(xe, *experts, layer, CAP_FACTOR * t // N_EXPERTS)
    return _combine(x1, sel, oe, offs, cnt, row(g_final), final_norm)


def _trunk(x, layers, experts, norm_mix_g, ssd_norm_g, attn_norm_g, norm_ffn_g, norm_final_g):
    b, l, _ = x.shape
    x2d = x.reshape(b * l, D_MODEL)
    depth = len(layers)
    for i, p in enumerate(layers):
        x2d = _layer(x2d, b, l, p, experts, i, norm_mix_g[i], ssd_norm_g[i], attn_norm_g[i],
                     norm_ffn_g[i], norm_final_g, i == depth - 1)
    return x2d.reshape(b, l, D_MODEL)


def kernel(x_prompt, x_sample, norm_mix_g, w_in, conv_w, conv_b, dt_bias, a_log, d_skip, ssd_norm_g, attn_norm_g, rpb, w_out, norm_ffn_g, w_router, w_gate, w_up, w_down, norm_final_g):
    depth = w_in.shape[0]
    layers = [_prep_layer(w_in[i], conv_w[i], conv_b[i], dt_bias[i], a_log[i], d_skip[i], rpb[i],
                          w_out[i], w_router[i])
              for i in range(depth)]
    experts = (w_gate.astype(BF16), w_up.astype(BF16), w_down.astype(BF16))
    run = functools.partial(_trunk, layers=layers, experts=experts,
                            norm_mix_g=norm_mix_g, ssd_norm_g=ssd_norm_g,
                            attn_norm_g=attn_norm_g, norm_ffn_g=norm_ffn_g,
                            norm_final_g=norm_final_g)
    return (run(x_prompt), run(x_sample))
```

```python
import functools

import jax
import jax.numpy as jnp
from jax import lax
from jax.experimental import pallas as pl
from jax.experimental.pallas import tpu as pltpu

F32 = jnp.float32
BF16 = jnp.bfloat16
U32 = jnp.uint32
I32 = jnp.int32

D_MODEL = 1024
D_SSD = 512
D_ATT = 512
N_HEADS = 8
CONV_W = 5
CONV_DIM = 768
CHUNK = 128
GRID_W = 64
KH = 8
KW = 16
N_EXPERTS = 16
CAP_FACTOR = 2
D_FF = 2048
RMS_EPS = 1e-6
NEG = -1e30
LOG2E = 1.4426950408889634

LANES = 128
HALO = 16
IN_COLS_PADDED = 2944

ROW_TILE = 1024
SSD_SEQS = 1
ATT_ROWS = 8
ATT_KROWS = 16
ATT_VISIT = 5
TOK_BLOCK = 512
SLOT_WIN = 128
SLOT_STEP = 120
SUBLANES = 8
FFN_TILE = 512
VMEM_LIMIT = 56 * 1024 * 1024


def _cparams(sem):
    return pltpu.CompilerParams(dimension_semantics=sem, vmem_limit_bytes=VMEM_LIMIT)


def _sigmoid(x):
    return 1.0 / (1.0 + jnp.exp(-x))


def _silu(x):
    return x * _sigmoid(x)


def _softplus(x):
    return jnp.maximum(x, 0.0) + jnp.log(1.0 + jnp.exp(-jnp.abs(x)))


def _rms(x, g):
    return x * lax.rsqrt(jnp.mean(x * x, axis=-1, keepdims=True) + RMS_EPS) * g


def _bits(x):
    return lax.bitcast_convert_type(x, U32)


def _pack_halves(x):
    lo = _bits(x[:, :512]) >> 16
    hi = _bits(x[:, 512:]) & jnp.uint32(0xFFFF0000)
    return hi | lo


def _unpack_halves(p):
    lo = lax.bitcast_convert_type(p << 16, F32)
    hi = lax.bitcast_convert_type(p & jnp.uint32(0xFFFF0000), F32)
    return jnp.concatenate([lo, hi], axis=1).astype(BF16)


def _in_proj_kernel(x_ref, xp_ref, xn_ref, g_ref, w_ref, cw_ref, cb_ref, dtb_ref, a_ref,
                    z_ref, act_ref, q_ref, k_ref, v_ref, dt_ref, cs_ref, xbc_ref,
                    *, tiles_per_seq):
    i = pl.program_id(0)
    g = g_ref[...]
    tm = x_ref.shape[0]
    u_all = jnp.concatenate([_rms(xp_ref[...], g).astype(BF16), _rms(x_ref[...], g).astype(BF16),
                             _rms(xn_ref[...], g).astype(BF16)], axis=0)
    u = u_all[HALO:HALO + tm]

    def proj(a, b):
        return jnp.dot(u, w_ref[:, a:b], preferred_element_type=F32)

    xbc_ref[...] = jnp.dot(u_all, w_ref[:, 512:1280], preferred_element_type=F32)
    first = lax.rem(i, tiles_per_seq) == 0
    last = lax.rem(i, tiles_per_seq) == tiles_per_seq - 1
    xbc_ref[:HALO, :] = jnp.where(first, 0.0, xbc_ref[:HALO, :])
    xbc_ref[HALO + tm:, :] = jnp.where(last, 0.0, xbc_ref[HALO + tm:, :])
    n_chunks = tm // CHUNK

    def conv_chunks(chunks):
        for c in chunks:
            for jb in range(CONV_DIM // LANES):
                cols = slice(LANES * jb, LANES * (jb + 1))
                xpad = xbc_ref[CHUNK * c:CHUNK * (c + 1) + 2 * HALO, cols]
                acc = jnp.broadcast_to(cb_ref[:, cols], (CHUNK, LANES))
                for k in range(CONV_W):
                    start = HALO - CONV_W // 2 + k
                    acc = acc + cw_ref[k:k + 1, cols] * xpad[start:start + CHUNK, :]
                act_ref[CHUNK * c:CHUNK * (c + 1), cols] = _silu(acc).astype(BF16)

    quarter = [range(n_chunks * n // 4, n_chunks * (n + 1) // 4) for n in range(4)]
    z_ref[...] = proj(0, 512).astype(BF16)
    conv_chunks(quarter[0])
    q_ref[...] = (proj(1280, 1792) * (64 ** -0.5 * LOG2E)).astype(BF16)
    conv_chunks(quarter[1])
    k_ref[...] = proj(1792, 2304).astype(BF16)
    conv_chunks(quarter[2])
    v_ref[...] = proj(2304, 2816).astype(BF16)
    conv_chunks(quarter[3])
    dtv = _softplus(proj(2816, 2944) + dtb_ref[...])
    dt_ref[...] = dtv
    a = dtv * a_ref[...]
    li = lax.broadcasted_iota(I32, (CHUNK, CHUNK), 0)
    si = lax.broadcasted_iota(I32, (CHUNK, CHUNK), 1)
    tri = jnp.where(si <= li, 1.0, 0.0).astype(F32)
    for c in range(n_chunks):
        rows = slice(CHUNK * c, CHUNK * (c + 1))
        cs_ref[rows, :] = jnp.dot(tri, a[rows], preferred_element_type=F32,
                                  precision=lax.Precision.HIGHEST)


def _in_proj(x2d, g, w, cw, cb, dtb, a_lane, seq_len):
    t = x2d.shape[0]
    tm = min(ROW_TILE, seq_len)
    assert seq_len % tm == 0 and tm % CHUNK == 0
    hb = tm // HALO
    row = lambda n: pl.BlockSpec((tm, n), lambda i: (i, 0))
    full = lambda a: pl.BlockSpec(a.shape, lambda i: (0, 0))
    prev = pl.BlockSpec((HALO, D_MODEL), lambda i: (jnp.maximum(i * hb - 1, 0), 0))
    nxt = pl.BlockSpec((HALO, D_MODEL), lambda i: (jnp.minimum((i + 1) * hb, t // HALO - 1), 0))
    return pl.pallas_call(
        functools.partial(_in_proj_kernel, tiles_per_seq=seq_len // tm),
        grid=(t // tm,),
        in_specs=[row(D_MODEL), prev, nxt, full(g), full(w), full(cw), full(cb), full(dtb),
                  full(a_lane)],
        out_specs=[row(512), row(CONV_DIM), row(512), row(512), row(512), row(LANES), row(LANES)],
        out_shape=[jax.ShapeDtypeStruct((t, 512), BF16),
                   jax.ShapeDtypeStruct((t, CONV_DIM), BF16),
                   jax.ShapeDtypeStruct((t, 512), BF16),
                   jax.ShapeDtypeStruct((t, 512), BF16),
                   jax.ShapeDtypeStruct((t, 512), BF16),
                   jax.ShapeDtypeStruct((t, LANES), F32),
                   jax.ShapeDtypeStruct((t, LANES), F32)],
        scratch_shapes=[pltpu.VMEM((tm + 2 * HALO, CONV_DIM), F32)],
        compiler_params=_cparams(("parallel",)),
    )(x2d, x2d, x2d, g, w, cw, cb, dtb, a_lane)


def _decay_terms(dt_ref, cs_ref, a_ref, s):
    dtv = dt_ref[s]
    cs = cs_ref[s]
    return dtv, cs, cs - dtv * a_ref[...], cs[CHUNK - 1:CHUNK, :]


def _pair_cols(x, h0, left):
    return jnp.where(left, x[:, h0:h0 + 1], x[:, h0 + 1:h0 + 2])


def _spread(v, sel_ref, k):
    return jnp.dot(v.astype(BF16), sel_ref[k], preferred_element_type=F32)


def _ssd_bwd_state_kernel(act_ref, dt_ref, cs_ref, sel_ref, a_ref, sin_ref, sb_ref):
    i = pl.program_id(1)

    @pl.when(i == 0)
    def _():
        sb_ref[...] = jnp.zeros_like(sb_ref)

    left = lax.broadcasted_iota(I32, (1, LANES), 1) < 64
    for s in range(SSD_SEQS):
        sin_ref[s, 0] = sb_ref[s]
        bt = act_ref[s, :, D_SSD:D_SSD + LANES].astype(F32).T.astype(BF16)
        dtv, _, ex, tot = _decay_terms(dt_ref, cs_ref, a_ref, s)
        wb = _spread(jnp.exp(ex) * dtv, sel_ref, 1)
        decb = jnp.exp(tot)
        for j in range(4):
            cols = slice(LANES * j, LANES * (j + 1))
            xw = (act_ref[s, :, cols].astype(F32) * wb[:, cols]).astype(BF16)
            st = jnp.dot(bt, xw, preferred_element_type=F32)
            sb_ref[s, j] = _pair_cols(decb, 8 + 2 * j, left) * sb_ref[s, j] + st


def _ssd_main_kernel(act_ref, dt_ref, cs_ref, z_ref, sin_ref, sel_ref, a_ref, dskip_ref, g_ref,
                     o_ref, sf_ref, y_ref):
    c = pl.program_id(1)

    @pl.when(c == 0)
    def _():
        sf_ref[...] = jnp.zeros_like(sf_ref)

    lane = lax.broadcasted_iota(I32, (1, LANES), 1)
    left = lane < 64
    li = lax.broadcasted_iota(I32, (CHUNK, CHUNK), 0)
    si = lax.broadcasted_iota(I32, (CHUNK, CHUNK), 1)
    for s in range(SSD_SEQS):
        bm_b = act_ref[s, :, D_SSD:D_SSD + LANES]
        cm_b = act_ref[s, :, D_SSD + LANES:]
        dtv, cs, ex, tot = _decay_terms(dt_ref, cs_ref, a_ref, s)
        cs_t, ex_t, dt_t = cs.T, ex.T, dtv.T
        wf = _spread(jnp.exp(tot - cs) * dtv, sel_ref, 0)
        ef = _spread(jnp.exp(cs), sel_ref, 0)
        eb = _spread(jnp.exp(tot - ex), sel_ref, 1)
        decf = jnp.exp(tot)
        bt = bm_b.astype(F32).T.astype(BF16)
        zero = jnp.zeros_like(bm_b)
        for grp in range(2):
            in_grp = left if grp == 0 else jnp.logical_not(left)
            b_g = jnp.where(in_grp, bm_b, zero)
            c_g = jnp.where(in_grp, cm_b, zero)
            cb_g = lax.dot_general(cm_b, b_g, (((1,), (1,)), ((), ())),
                                   preferred_element_type=F32)
            for jj in range(2):
                j = 2 * grp + jj
                cols = slice(LANES * j, LANES * (j + 1))
                x_pb = act_ref[s, :, cols]
                x_p = x_pb.astype(F32)
                halves = []
                for sub in range(2):
                    h = 2 * j + sub
                    dtf = dt_t[h:h + 1, :]
                    dtb = dt_t[8 + h:9 + h, :]
                    arg = jnp.where(si <= li,
                                    cs[:, h:h + 1] - cs_t[h:h + 1, :],
                                    ex_t[8 + h:9 + h, :] - ex[:, 8 + h:9 + h])
                    dsel = jnp.where(si < li, dtf, jnp.where(si > li, dtb, dtf + dtb))
                    m = (cb_g * jnp.exp(arg) * dsel).astype(BF16)
                    halves.append(jnp.dot(m, x_pb, preferred_element_type=F32))
                y = jnp.where(left, halves[0], halves[1])
                sf = sf_ref[s, j]
                sb = sin_ref[s, 0, j]
                y = y + jnp.dot(c_g, sf.astype(BF16), preferred_element_type=F32) * ef[:, cols]
                y = y + jnp.dot(c_g, sb.astype(BF16), preferred_element_type=F32) * eb[:, cols]
                y_ref[s, :, cols] = y + dskip_ref[:, cols] * x_p
                xw = (x_p * wf[:, cols]).astype(BF16)
                st = jnp.dot(bt, xw, preferred_element_type=F32)
                sf_ref[s, j] = _pair_cols(decf, 2 * j, left) * sf + st

        yz = y_ref[s] * _silu(z_ref[s].astype(F32))
        o_ref[s] = _rms(yz, g_ref[...]).astype(BF16)


def _ssd(act, dt, cs, z, a_lane, dskip, g):
    b, l, _ = act.shape
    nc = l // CHUNK
    assert b % SSD_SEQS == 0
    ns = SSD_SEQS
    rev = lambda i: nc - 1 - i
    const = lambda a: pl.BlockSpec(a.shape, lambda bi, i: (0,) * a.ndim)
    state = lambda chunk_of: pl.BlockSpec((ns, 1, 4, LANES, LANES),
                                          lambda bi, i: (bi, chunk_of(i), 0, 0, 0))
    chunk = lambda n, chunk_of: pl.BlockSpec((ns, CHUNK, n), lambda bi, i: (bi, chunk_of(i), 0))
    head_of_lane = jnp.arange(D_SSD)[None, :] // 64
    row = jnp.arange(LANES)[:, None]
    sel = jnp.stack([(row == head_of_lane + N_HEADS * k) for k in range(2)]).astype(BF16)

    s_in = pl.pallas_call(
        _ssd_bwd_state_kernel,
        grid=(b // ns, nc),
        in_specs=[chunk(CONV_DIM, rev), chunk(LANES, rev), chunk(LANES, rev), const(sel),
                  const(a_lane)],
        out_specs=state(rev),
        out_shape=jax.ShapeDtypeStruct((b, nc, 4, LANES, LANES), F32),
        scratch_shapes=[pltpu.VMEM((ns, 4, LANES, LANES), F32)],
        compiler_params=_cparams(("parallel", "arbitrary")),
    )(act, dt, cs, sel, a_lane)

    fwd = lambda i: i
    return pl.pallas_call(
        _ssd_main_kernel,
        grid=(b // ns, nc),
        in_specs=[chunk(CONV_DIM, fwd), chunk(LANES, fwd), chunk(LANES, fwd), chunk(D_SSD, fwd),
                  state(fwd), const(sel), const(a_lane), const(dskip), const(g)],
        out_specs=chunk(D_SSD, fwd),
        out_shape=jax.ShapeDtypeStruct((b, l, D_SSD), BF16),
        scratch_shapes=[pltpu.VMEM((ns, 4, LANES, LANES), F32),
                        pltpu.VMEM((ns, CHUNK, D_SSD), F32)],
        compiler_params=_cparams(("parallel", "arbitrary")),
    )(act, dt, cs, z, s_in, sel, a_lane, dskip, g)


def _att_kernel(q_ref, k0_ref, k1_ref, k2_ref, k3_ref, v0_ref, v1_ref, v2_ref, v3_ref,
                t2_ref, g_ref, o_ref, s_ref, t_ref, p_ref, linv_ref, ov_ref, acc_ref, *, rows):
    blk = pl.program_id(1)
    lane = lax.broadcasted_iota(I32, (1, LANES), 1)
    left = lane < 64
    n_pairs = ATT_KROWS // 2
    zero_tile = jnp.zeros((GRID_W, LANES), BF16)

    def softmax_tail(par, rq, mx, kcols_of, n_tiles):
        m = jnp.max(mx, axis=1, keepdims=True)
        lsum = jnp.zeros((GRID_W, LANES), F32)
        for d in range(n_tiles):
            p = jnp.exp2(t_ref[par, rq, LANES * d:LANES * (d + 1)] - m)
            lsum = lsum + p
            p_ref[par, rq, kcols_of(d)] = p.astype(BF16)
        tot = jnp.sum(lsum, axis=1, keepdims=True)
        linv_ref[par, rq, :] = jnp.broadcast_to(1.0 / tot, (GRID_W, LANES))

    def window(i, place):
        koff = {"top": 0, "interior": -(KH // 2), "bottom": -KH}[place]
        js = {"top": max(i - KH // 2, 0), "interior": i, "bottom": min(i + KH // 2, KH)}[place]
        return koff, js, js // 2, (js + KH - 1) // 2

    def softmax_row(i, h, par, place):
        koff, js, first, last = window(i, place)
        rq = slice(GRID_W * i, GRID_W * (i + 1))
        kcols_of = lambda d: slice(LANES * (first + d), LANES * (first + d + 1))
        mx = None
        for d in range(last - first + 1):
            jl = 2 * (first + d)
            t = s_ref[par, rq, kcols_of(d)] + t2_ref[h, koff + jl - i + KH]
            if jl < js:
                t = jnp.where(left, NEG, t)
            if jl + 1 > js + KH - 1:
                t = jnp.where(left, t, NEG)
            t_ref[par, rq, LANES * d:LANES * (d + 1)] = t
            mx = t if mx is None else jnp.maximum(mx, t)
        softmax_tail(par, rq, mx, kcols_of, last - first + 1)
        for jp in range(2 * (first // 2), 2 * (last // 2) + 2):
            if jp < first or jp > last:
                p_ref[par, rq, LANES * jp:LANES * (jp + 1)] = zero_tile

    def all_heads(place):
        kt_rows = []
        for kt in range(n_pairs // 2):
            vis = [i for i in range(ATT_ROWS)
                   if window(i, place)[2] // 2 <= kt <= window(i, place)[3] // 2]
            if vis:
                kt_rows.append((kt, vis[0], vis[-1] + 1))
        k_refs = (k0_ref, k1_ref, k2_ref, k3_ref)
        v_refs = (v0_ref, v1_ref, v2_ref, v3_ref)
        for hp in range(N_HEADS // 2):
            cols = slice(LANES * hp, LANES * (hp + 1))
            o_pair = None
            for sub in range(2):
                h = 2 * hp + sub
                in_head = left if sub == 0 else jnp.logical_not(left)
                for kt, ra, rb in kt_rows:
                    keys = slice(2 * LANES * kt, 2 * LANES * (kt + 1))
                    k_tile = k_refs[kt][0, :, cols]
                    s_ref[sub, GRID_W * ra:GRID_W * rb, keys] = lax.dot_general(
                        q_ref[0, GRID_W * ra:GRID_W * rb, cols],
                        jnp.where(in_head, k_tile, jnp.zeros_like(k_tile)),
                        (((1,), (1,)), ((), ())), preferred_element_type=F32)
                for i in range(ATT_ROWS):
                    softmax_row(i, h, sub, place)
                done = 0
                for kt, ra, rb in kt_rows:
                    keys = slice(2 * LANES * kt, 2 * LANES * (kt + 1))
                    part = jnp.dot(p_ref[sub, GRID_W * ra:GRID_W * rb, keys],
                                   v_refs[kt][0, :, cols], preferred_element_type=F32)
                    old = max(min(rb, done) - ra, 0)
                    if old:
                        ov_ref[sub, GRID_W * ra:GRID_W * (ra + old), :] += part[:GRID_W * old]
                    if ra + old < rb:
                        ov_ref[sub, GRID_W * (ra + old):GRID_W * rb, :] = part[GRID_W * old:]
                    done = max(done, rb)
                o_full = ov_ref[sub] * linv_ref[sub]
                o_pair = o_full if sub == 0 else jnp.where(left, o_pair, o_full)
            acc_ref[:, cols] = o_pair

    last_blk = rows // ATT_ROWS - 1
    pl.when(blk == 0)(lambda: all_heads("top"))
    pl.when(jnp.logical_and(blk > 0, blk < last_blk))(lambda: all_heads("interior"))
    pl.when(blk == last_blk)(lambda: all_heads("bottom"))
    o_ref[0] = _rms(acc_ref[...], g_ref[...]).astype(BF16)


def _attention(q, k, v, t2, g):
    b, l, _ = q.shape
    rows = l // GRID_W
    assert rows >= ATT_KROWS and rows % ATT_ROWS == 0
    nblk = rows // ATT_ROWS
    nq = ATT_ROWS * GRID_W
    kb = 4 * GRID_W
    n_kb = l // kb

    def kspec(m):
        return pl.BlockSpec(
            (1, kb, D_ATT),
            lambda bi, i, m=m: (bi, jnp.clip(2 * i - 1, 0, n_kb - 4) + m, 0))

    const = lambda a: pl.BlockSpec(a.shape, lambda bi, i: (0,) * a.ndim)
    return pl.pallas_call(
        functools.partial(_att_kernel, rows=rows),
        grid=(b, nblk),
        in_specs=[pl.BlockSpec((1, nq, D_ATT), lambda bi, i: (bi, i, 0))]
                 + [kspec(m) for m in range(4)] + [kspec(m) for m in range(4)]
                 + [const(t2), const(g)],
        out_specs=pl.BlockSpec((1, nq, D_ATT), lambda bi, i: (bi, i, 0)),
        out_shape=jax.ShapeDtypeStruct((b, l, D_ATT), BF16),
        scratch_shapes=[pltpu.VMEM((2, nq, ATT_KROWS * GRID_W), F32),
                        pltpu.VMEM((2, nq, ATT_VISIT * LANES), F32),
                        pltpu.VMEM((2, nq, ATT_KROWS * GRID_W), BF16),
                        pltpu.VMEM((2, nq, LANES), F32),
                        pltpu.VMEM((2, nq, LANES), F32),
                        pltpu.VMEM((nq, D_ATT), F32)],
        compiler_params=_cparams(("parallel", "parallel")),
    )(q, k, k, k, k, v, v, v, v, t2, g)


def _bias_tables(rpb):
    c = jnp.arange(GRID_W)
    col_start = jnp.clip(c - KW // 2, 0, GRID_W - KW)
    col_mask = (c[None, :] >= col_start[:, None]) & (c[None, :] < col_start[:, None] + KW)
    dc = jnp.clip(c[None, :] - c[:, None] + (KW - 1), 0, 2 * KW - 2)
    t = rpb.astype(F32)[:, :, dc]
    t = jnp.where(col_mask[None, None], t * LOG2E, NEG)
    pad = jnp.full((N_HEADS, 1, GRID_W, GRID_W), NEG, F32)
    t = jnp.concatenate([pad, t, pad], axis=1)
    return jnp.concatenate([t[:, :-1], t[:, 1:]], axis=-1)


def _out_proj_kernel(x_ref, ys_ref, ya_ref, wo_ref, g_ref, wr_ref, x1_ref, u_ref, aff_ref):
    x1 = x_ref[...] \
        + jnp.dot(ys_ref[...], wo_ref[:D_SSD, :], preferred_element_type=F32) \
        + jnp.dot(ya_ref[...], wo_ref[D_SSD:, :], preferred_element_type=F32)
    x1_ref[...] = x1
    ub = _rms(x1, g_ref[...]).astype(BF16)
    u_ref[...] = ub
    logits = jnp.dot(ub, wr_ref[...], preferred_element_type=F32)
    lt = logits.T[:N_EXPERTS, :]
    e = jnp.exp(lt - jnp.max(lt, axis=0, keepdims=True))
    aff_ref[...] = e / jnp.sum(e, axis=0, keepdims=True)


def _out_proj(x2d, ys, ya, wo, g, wr):
    t = x2d.shape[0]
    tm = min(ROW_TILE, t)
    row = lambda n: pl.BlockSpec((tm, n), lambda i: (i, 0))
    full = lambda a: pl.BlockSpec(a.shape, lambda i: (0, 0))
    return pl.pallas_call(
        _out_proj_kernel,
        grid=(t // tm,),
        in_specs=[row(D_MODEL), row(D_SSD), row(D_ATT), full(wo), full(g), full(wr)],
        out_specs=[row(D_MODEL), row(D_MODEL), pl.BlockSpec((N_EXPERTS, tm), lambda i: (0, i))],
        out_shape=[jax.ShapeDtypeStruct((t, D_MODEL), F32),
                   jax.ShapeDtypeStruct((t, D_MODEL), BF16),
                   jax.ShapeDtypeStruct((N_EXPERTS, t), F32)],
        compiler_params=_cparams(("parallel",)),
    )(x2d, ys, ya, wo, g, wr)


def _route_kernel(aff_ref, sel_ref, offs_ref, cnt_ref, *, cap, nb):
    aff = aff_ref[...]
    t = aff.shape[1]
    keys = lax.bitcast_convert_type(aff, I32)
    capf = jnp.float32(cap)

    def count(mask):
        return jnp.sum(jnp.where(mask, 1.0, 0.0), axis=1, keepdims=True)

    def key_step(i, cur):
        cand = cur | lax.shift_left(jnp.int32(1), 30 - i)
        return jnp.where(count(keys >= cand) >= capf, cand, cur)

    thr = lax.fori_loop(0, 31, key_step, jnp.zeros((N_EXPERTS, 1), I32))
    gt = keys > thr
    eq = keys == thr
    need = capf - count(gt)
    idx = lax.broadcasted_iota(I32, (N_EXPERTS, t), 1)
    nbits = max(t.bit_length(), 1)

    def idx_step(i, cur):
        cand = cur | lax.shift_left(jnp.int32(1), nbits - 1 - i)
        return jnp.where(count(jnp.logical_and(eq, idx < cand)) < need, cand, cur)

    last = lax.fori_loop(0, nbits, idx_step, jnp.zeros((N_EXPERTS, 1), I32))
    sel = jnp.where(gt, 1.0, jnp.where(jnp.logical_and(eq, idx <= last), 1.0, 0.0))
    sel_ref[...] = sel

    lane = lax.broadcasted_iota(I32, (N_EXPERTS, LANES), 1)
    cnt = jnp.zeros((N_EXPERTS, LANES), F32)
    for b in range(nb):
        cb = jnp.sum(sel[:, TOK_BLOCK * b:TOK_BLOCK * (b + 1)], axis=1, keepdims=True)
        cnt = jnp.where(lane == b, cb, cnt)
    incl = cnt
    sh = 1
    while sh < LANES:
        incl = incl + jnp.where(lane >= sh, pltpu.roll(incl, sh, axis=1), 0.0)
        sh *= 2
    offs_ref[...] = (incl - cnt).astype(I32)
    cnt_ref[...] = cnt.astype(I32)


def _route(aff_t):
    t = aff_t.shape[1]
    cap = CAP_FACTOR * t // N_EXPERTS
    nb = t // TOK_BLOCK
    assert nb <= LANES
    return pl.pallas_call(
        functools.partial(_route_kernel, cap=cap, nb=nb),
        out_shape=[jax.ShapeDtypeStruct((N_EXPERTS, t), F32),
                   jax.ShapeDtypeStruct((N_EXPERTS, LANES), I32),
                   jax.ShapeDtypeStruct((N_EXPERTS, LANES), I32)],
        compiler_params=pltpu.CompilerParams(vmem_limit_bytes=VMEM_LIMIT),
    )(aff_t)


def _slot_positions(sel):
    si = lax.broadcasted_iota(I32, (TOK_BLOCK, TOK_BLOCK), 0)
    ti = lax.broadcasted_iota(I32, (TOK_BLOCK, TOK_BLOCK), 1)
    tri = jnp.where(si <= ti, 1.0, 0.0).astype(BF16)
    incl = jnp.dot(sel.astype(BF16), tri, preferred_element_type=F32)
    return jnp.where(sel > 0.0, incl - 1.0, -1e6)


def _num_chunks(cnt_ref, b):
    mx = cnt_ref[0, b]
    for e in range(1, N_EXPERTS):
        mx = jnp.maximum(mx, cnt_ref[e, b])
    return (mx + SLOT_STEP - 1) // SLOT_STEP


def _align_down(v):
    return pl.multiple_of((v // SUBLANES) * SUBLANES, SUBLANES)


def _dispatch_kernel(offs_ref, cnt_ref, u_ref, sel_ref, aff_ref, xe_hbm,
                     g_ref, stage_ref, carry_ref, sem, n_ref, *, nb, cap):
    b = pl.program_id(0)

    def copy(slot, e, row):
        return pltpu.make_async_copy(stage_ref.at[slot, e],
                                     xe_hbm.at[e, pl.ds(row, SLOT_WIN), :], sem.at[slot])

    @pl.when(b == 0)
    def _():
        n_ref[0] = 0
        carry_ref[...] = jnp.zeros_like(carry_ref)
        stage_ref[1, 0] = jnp.zeros((SLOT_WIN, 640), U32)
        for e in range(N_EXPERTS):
            pltpu.make_async_copy(stage_ref.at[1, 0], xe_hbm.at[e, pl.ds(cap, SLOT_WIN), :],
                                  sem.at[1]).start()
        for e in range(N_EXPERTS):
            copy(1, 0, 0).wait()

    posm = _slot_positions(sel_ref[...])
    aff = aff_ref[...]
    ub = u_ref[...]
    w_iota = lax.broadcasted_iota(I32, (SLOT_WIN, TOK_BLOCK), 0).astype(F32)
    head_row = lax.broadcasted_iota(I32, (SUBLANES, 640), 0)

    def chunk(k, carry):
        n = n_ref[0]
        slot = lax.rem(n, 2)
        win = []
        for e in range(N_EXPERTS):
            cnt = cnt_ref[e, b]
            done = jnp.minimum(k * SLOT_STEP, cnt)
            base = offs_ref[e, b] + done
            row = _align_down(base)
            end = base + jnp.minimum((k + 1) * SLOT_STEP, cnt) - done
            win.append((row, base - row, _align_down(end) - row))
            ge = posm[e:e + 1, :] == w_iota + (done - (base - row)).astype(F32)
            g_ref[SLOT_WIN * e:SLOT_WIN * (e + 1), :] = jnp.where(ge, 1.0, 0.0).astype(BF16)
            gate = jnp.sum(jnp.where(ge, aff[e:e + 1, :], 0.0), axis=1, keepdims=True)
            stage_ref[slot, e, :, 512:] = _bits(jnp.broadcast_to(gate, (SLOT_WIN, LANES)))
        packed = _pack_halves(jnp.dot(g_ref[...], ub, preferred_element_type=F32))
        for e in range(N_EXPERTS):
            row, skew, nxt = win[e]
            stage_ref[slot, e, :, :512] = packed[SLOT_WIN * e:SLOT_WIN * (e + 1), :]
            stage_ref[slot, e, :SUBLANES, :] = jnp.where(
                head_row < skew, carry_ref[e], stage_ref[slot, e, :SUBLANES, :])
            carry_ref[e] = stage_ref[slot, e, pl.ds(pl.multiple_of(nxt, SUBLANES), SUBLANES), :]

        @pl.when(n > 0)
        def _():
            for e in range(N_EXPERTS):
                copy(1 - slot, e, 0).wait()

        for e in range(N_EXPERTS):
            copy(slot, e, win[e][0]).start()
        n_ref[0] = n + 1
        return carry

    lax.fori_loop(0, _num_chunks(cnt_ref, b), chunk, 0)

    @pl.when(jnp.logical_and(b == nb - 1, n_ref[0] > 0))
    def _():
        slot = lax.rem(n_ref[0] - 1, 2)
        for e in range(N_EXPERTS):
            copy(slot, e, 0).wait()


def _dispatch(u, sel, aff_t, offs, cnt):
    t = u.shape[0]
    cap = CAP_FACTOR * t // N_EXPERTS
    nb = t // TOK_BLOCK
    cap_p = cap + SLOT_WIN
    grid_spec = pltpu.PrefetchScalarGridSpec(
        num_scalar_prefetch=2,
        grid=(nb,),
        in_specs=[pl.BlockSpec((TOK_BLOCK, D_MODEL), lambda b, *_: (b, 0)),
                  pl.BlockSpec((N_EXPERTS, TOK_BLOCK), lambda b, *_: (0, b)),
                  pl.BlockSpec((N_EXPERTS, TOK_BLOCK), lambda b, *_: (0, b))],
        out_specs=pl.BlockSpec(memory_space=pl.ANY),
        scratch_shapes=[pltpu.VMEM((N_EXPERTS * SLOT_WIN, TOK_BLOCK), BF16),
                        pltpu.VMEM((2, N_EXPERTS, SLOT_WIN, 640), U32),
                        pltpu.VMEM((N_EXPERTS, SUBLANES, 640), U32),
                        pltpu.SemaphoreType.DMA((2,)),
                        pltpu.SMEM((1,), I32)])
    return pl.pallas_call(
        functools.partial(_dispatch_kernel, nb=nb, cap=cap),
        grid_spec=grid_spec,
        out_shape=jax.ShapeDtypeStruct((N_EXPERTS, cap_p, 640), U32),
        compiler_params=_cparams(("arbitrary",)),
    )(offs, cnt, u, sel, aff_t)


def _ffn_kernel(xe_ref, wg_ref, wu_ref, wd_ref, o_ref):
    xp = xe_ref[0]
    x = _unpack_halves(xp[:, :512])
    gate = lax.bitcast_convert_type(xp[:, 512:513], F32)
    gp = jnp.dot(x, wg_ref[0, 0], preferred_element_type=F32)
    up = jnp.dot(x, wu_ref[0, 0], preferred_element_type=F32)
    hid = (_silu(gp) * up).astype(BF16)
    out = jnp.dot(hid, wd_ref[0, 0], preferred_element_type=F32) * gate
    o_ref[0] = _pack_halves(out.astype(BF16).astype(F32))


def _ffn(xe, wg, wu, wd, layer, cap):
    tm = min(FFN_TILE, cap)
    return pl.pallas_call(
        _ffn_kernel,
        grid=(N_EXPERTS, cap // tm),
        in_specs=[pl.BlockSpec((1, tm, 640), lambda e, j: (e, j, 0)),
                  pl.BlockSpec((1, 1, D_MODEL, D_FF), lambda e, j: (layer, e, 0, 0)),
                  pl.BlockSpec((1, 1, D_MODEL, D_FF), lambda e, j: (layer, e, 0, 0)),
                  pl.BlockSpec((1, 1, D_FF, D_MODEL), lambda e, j: (layer, e, 0, 0))],
        out_specs=pl.BlockSpec((1, tm, 512), lambda e, j: (e, j, 0)),
        out_shape=jax.ShapeDtypeStruct((N_EXPERTS, cap, 512), U32),
        compiler_params=_cparams(("parallel", "parallel")),
    )(xe, wg, wu, wd)


def _combine_kernel(offs_ref, cnt_ref, x1_ref, sel_ref, gfin_ref, oe_hbm, o_ref,
                    g_ref, stage_ref, acc_ref, sem, *, nb, cap, final_norm):
    b = pl.program_id(0)
    w_iota = lax.broadcasted_iota(I32, (SLOT_WIN, TOK_BLOCK), 0).astype(F32)

    def window(bb, k, e):
        done = jnp.minimum(k * SLOT_STEP, cnt_ref[e, bb])
        base = offs_ref[e, bb] + done
        row = jnp.minimum(_align_down(base), cap - SLOT_WIN)
        return row, done - (base - row)

    def copy(slot, e, row):
        return pltpu.make_async_copy(oe_hbm.at[e, pl.ds(row, SLOT_WIN), :],
                                     stage_ref.at[slot, pl.ds(SLOT_WIN * e, SLOT_WIN), :],
                                     sem.at[slot])

    def fetch(bb, k, slot):
        for e in range(N_EXPERTS):
            copy(slot, e, pl.multiple_of(window(bb, k, e)[0], SUBLANES)).start()

    def wait(slot):
        for e in range(N_EXPERTS):
            copy(slot, e, 0).wait()

    @pl.when(b == 0)
    def _():
        fetch(0, 0, 0)

    @pl.when(b + 1 < nb)
    def _():
        fetch(b + 1, 0, lax.rem(b + 1, 2))

    pos_all = _slot_positions(sel_ref[...])

    def expand(slot, k):
        lo = (k * SLOT_STEP).astype(F32)
        posm = jnp.where(jnp.logical_and(pos_all >= lo, pos_all < lo + SLOT_STEP), pos_all, -1e6)
        for e in range(N_EXPERTS):
            wi = w_iota + window(b, k, e)[1].astype(F32)
            g_ref[SLOT_WIN * e:SLOT_WIN * (e + 1), :] = jnp.where(posm[e:e + 1, :] == wi, 1.0, 0.0)
        gt = g_ref[...].T.astype(BF16)
        slab = _unpack_halves(stage_ref[slot])
        return jnp.dot(gt, slab, preferred_element_type=F32)

    wait(lax.rem(b, 2))
    acc_ref[...] = x1_ref[...] + expand(lax.rem(b, 2), jnp.int32(0))

    def extra(k, carry):
        fetch(b, k, 2)
        wait(2)
        acc_ref[...] += expand(2, k)
        return carry

    lax.fori_loop(1, _num_chunks(cnt_ref, b), extra, 0)
    y = acc_ref[...]
    o_ref[...] = _rms(y, gfin_ref[...]) if final_norm else y


def _combine(x1, sel, oe, offs, cnt, gfin, final_norm):
    t = x1.shape[0]
    cap = CAP_FACTOR * t // N_EXPERTS
    nb = t // TOK_BLOCK
    grid_spec = pltpu.PrefetchScalarGridSpec(
        num_scalar_prefetch=2,
        grid=(nb,),
        in_specs=[pl.BlockSpec((TOK_BLOCK, D_MODEL), lambda b, *_: (b, 0)),
                  pl.BlockSpec((N_EXPERTS, TOK_BLOCK), lambda b, *_: (0, b)),
                  pl.BlockSpec((1, D_MODEL), lambda b, *_: (0, 0)),
                  pl.BlockSpec(memory_space=pl.ANY)],
        out_specs=pl.BlockSpec((TOK_BLOCK, D_MODEL), lambda b, *_: (b, 0)),
        scratch_shapes=[pltpu.VMEM((N_EXPERTS * SLOT_WIN, TOK_BLOCK), F32),
                        pltpu.VMEM((3, N_EXPERTS * SLOT_WIN, 512), U32),
                        pltpu.VMEM((TOK_BLOCK, D_MODEL), F32),
                        pltpu.SemaphoreType.DMA((3,))])
    return pl.pallas_call(
        functools.partial(_combine_kernel, nb=nb, cap=cap, final_norm=final_norm),
        grid_spec=grid_spec,
        out_shape=jax.ShapeDtypeStruct((t, D_MODEL), F32),
        compiler_params=_cparams(("arbitrary",)),
    )(offs, cnt, x1, sel, gfin, oe)


def _prep_layer(w_in, conv_w, conv_b, dt_bias, a_log, d_skip, rpb, w_out, w_router):
    o1 = D_SSD
    o2 = o1 + CONV_DIM
    o3 = o2 + 2 * N_HEADS
    w = jnp.concatenate(
        [w_in[:, :o2], w_in[:, o3:], w_in[:, o2:o3],
         jnp.zeros((D_MODEL, LANES - 2 * N_HEADS), w_in.dtype)], axis=1).astype(BF16)
    lane_pad = lambda v: jnp.concatenate([v.reshape(-1).astype(F32),
                                          jnp.zeros((LANES - 2 * N_HEADS,), F32)])[None, :]
    return dict(
        w_in=w,
        conv_w=jnp.concatenate([conv_w.astype(F32), jnp.zeros((8 - CONV_W, CONV_DIM), F32)], 0),
        conv_b=conv_b.astype(F32)[None, :],
        dt_bias=lane_pad(dt_bias),
        a_lane=lane_pad(-jnp.exp(a_log.astype(F32))),
        d_skip=jnp.repeat(d_skip.astype(F32), 64)[None, :],
        t2=_bias_tables(rpb),
        w_out=w_out.astype(BF16),
        w_router=jnp.concatenate(
            [w_router, jnp.zeros((D_MODEL, LANES - N_EXPERTS), w_router.dtype)], 1).astype(BF16))


def _layer(x2d, b, l, p, experts, layer, g_mix, g_ssd, g_att, g_ffn, g_final, final_norm):
    t = b * l
    row = lambda v: v.astype(F32)[None, :]
    z, act, q, k, v, dt, cs = _in_proj(x2d, row(g_mix), p["w_in"], p["conv_w"], p["conv_b"],
                                       p["dt_bias"], p["a_lane"], l)
    r3 = lambda a: a.reshape(b, l, a.shape[-1])
    y_ssd = _ssd(r3(act), r3(dt), r3(cs), r3(z), p["a_lane"], p["d_skip"], row(g_ssd))
    y_att = _attention(r3(q), r3(k), r3(v), p["t2"], row(g_att))
    x1, u, aff_t = _out_proj(x2d, y_ssd.reshape(t, D_SSD), y_att.reshape(t, D_ATT),
                             p["w_out"], row(g_ffn), p["w_router"])
    sel, offs, cnt = _route(aff_t)
    xe = _dispatch(u, sel, aff_t, offs, cnt)
    oe = _ffn(xe, *experts, layer, CAP_FACTOR * t // N_EXPERTS)
    return _combine(x1, sel, oe, offs, cnt, row(g_final), final_norm)


def _trunk(x, layers, experts, norm_mix_g, ssd_norm_g, attn_norm_g, norm_ffn_g, norm_final_g):
    b, l, _ = x.shape
    x2d = x.reshape(b * l, D_MODEL)
    depth = len(layers)
    for i, p in enumerate(layers):
        x2d = _layer(x2d, b, l, p, experts, i, norm_mix_g[i], ssd_norm_g[i], attn_norm_g[i],
                     norm_ffn_g[i], norm_final_g, i == depth - 1)
    return x2d.reshape(b, l, D_MODEL)


def kernel(x_prompt, x_sample, norm_mix_g, w_in, conv_w, conv_b, dt_bias, a_log, d_skip, ssd_norm_g, attn_norm_g, rpb, w_out, norm_ffn_g, w_router, w_gate, w_up, w_down, norm_final_g):
    depth = w_in.shape[0]
    layers = [_prep_layer(w_in[i], conv_w[i], conv_b[i], dt_bias[i], a_log[i], d_skip[i], rpb[i],
                          w_out[i], w_router[i])
              for i in range(depth)]
    experts = (w_gate.astype(BF16), w_up.astype(BF16), w_down.astype(BF16))
    run = functools.partial(_trunk, layers=layers, experts=experts,
                            norm_mix_g=norm_mix_g, ssd_norm_g=ssd_norm_g,
                            attn_norm_g=attn_norm_g, norm_ffn_g=norm_ffn_g,
                            norm_final_g=norm_final_g)
    return (run(x_prompt), run(x_sample))
```

```python
import functools

import jax
import jax.numpy as jnp
from jax import lax
from jax.experimental import pallas as pl
from jax.experimental.pallas import tpu as pltpu

F32 = jnp.float32
BF16 = jnp.bfloat16
U32 = jnp.uint32
I32 = jnp.int32

D_MODEL = 1024
D_SSD = 512
D_ATT = 512
N_HEADS = 8
CONV_W = 5
CONV_DIM = 768
CHUNK = 128
GRID_W = 64
KH = 8
KW = 16
N_EXPERTS = 16
CAP_FACTOR = 2
D_FF = 2048
RMS_EPS = 1e-6
NEG = -1e30
LOG2E = 1.4426950408889634

LANES = 128
HALO = 16
IN_COLS_PADDED = 2944

ROW_TILE = 1024
SSD_CHUNKS = 4
ATT_ROWS = 8
ATT_KROWS = 16
ATT_VISIT = 5
TOK_BLOCK = 512
SLOT_WIN = 128
SLOT_STEP = 120
SUBLANES = 8
FFN_TILE = 512
VMEM_LIMIT = 56 * 1024 * 1024


def _cparams(sem):
    return pltpu.CompilerParams(dimension_semantics=sem, vmem_limit_bytes=VMEM_LIMIT)


def _sigmoid(x):
    return 1.0 / (1.0 + jnp.exp(-x))


def _silu(x):
    return x * _sigmoid(x)


def _softplus(x):
    return jnp.maximum(x, 0.0) + jnp.log(1.0 + jnp.exp(-jnp.abs(x)))


def _rms(x, g):
    return x * lax.rsqrt(jnp.mean(x * x, axis=-1, keepdims=True) + RMS_EPS) * g


def _bits(x):
    return lax.bitcast_convert_type(x, U32)


def _pack_halves(x):
    lo = _bits(x[:, :512]) >> 16
    hi = _bits(x[:, 512:]) & jnp.uint32(0xFFFF0000)
    return hi | lo


def _unpack_halves(p):
    lo = lax.bitcast_convert_type(p << 16, F32)
    hi = lax.bitcast_convert_type(p & jnp.uint32(0xFFFF0000), F32)
    return jnp.concatenate([lo, hi], axis=1).astype(BF16)


def _in_proj_kernel(x_ref, xp_ref, xn_ref, g_ref, w_ref, cw_ref, cb_ref, dtb_ref, a_ref,
                    z_ref, act_ref, q_ref, k_ref, v_ref, dt_ref, cs_ref, xbc_ref,
                    *, tiles_per_seq):
    i = pl.program_id(0)
    g = g_ref[...]
    tm = x_ref.shape[0]
    u_all = jnp.concatenate([_rms(xp_ref[...], g).astype(BF16), _rms(x_ref[...], g).astype(BF16),
                             _rms(xn_ref[...], g).astype(BF16)], axis=0)
    u = u_all[HALO:HALO + tm]

    def proj(a, b):
        return jnp.dot(u, w_ref[:, a:b], preferred_element_type=F32)

    xbc_ref[...] = jnp.dot(u_all, w_ref[:, 512:1280], preferred_element_type=F32)
    first = lax.rem(i, tiles_per_seq) == 0
    last = lax.rem(i, tiles_per_seq) == tiles_per_seq - 1
    xbc_ref[:HALO, :] = jnp.where(first, 0.0, xbc_ref[:HALO, :])
    xbc_ref[HALO + tm:, :] = jnp.where(last, 0.0, xbc_ref[HALO + tm:, :])
    n_chunks = tm // CHUNK

    def conv_chunks(chunks):
        for c in chunks:
            for jb in range(CONV_DIM // LANES):
                cols = slice(LANES * jb, LANES * (jb + 1))
                xpad = xbc_ref[CHUNK * c:CHUNK * (c + 1) + 2 * HALO, cols]
                acc = jnp.broadcast_to(cb_ref[:, cols], (CHUNK, LANES))
                for k in range(CONV_W):
                    start = HALO - CONV_W // 2 + k
                    acc = acc + cw_ref[k:k + 1, cols] * xpad[start:start + CHUNK, :]
                act_ref[CHUNK * c:CHUNK * (c + 1), cols] = _silu(acc).astype(BF16)

    quarter = [range(n_chunks * n // 4, n_chunks * (n + 1) // 4) for n in range(4)]
    z_ref[...] = proj(0, 512).astype(BF16)
    conv_chunks(quarter[0])
    q_ref[...] = (proj(1280, 1792) * (64 ** -0.5 * LOG2E)).astype(BF16)
    conv_chunks(quarter[1])
    k_ref[...] = proj(1792, 2304).astype(BF16)
    conv_chunks(quarter[2])
    v_ref[...] = proj(2304, 2816).astype(BF16)
    conv_chunks(quarter[3])
    dtv = _softplus(proj(2816, 2944) + dtb_ref[...])
    dt_ref[...] = dtv
    a = dtv * a_ref[...]
    li = lax.broadcasted_iota(I32, (CHUNK, CHUNK), 0)
    si = lax.broadcasted_iota(I32, (CHUNK, CHUNK), 1)
    tri = jnp.where(si <= li, 1.0, 0.0).astype(F32)
    for c in range(n_chunks):
        rows = slice(CHUNK * c, CHUNK * (c + 1))
        cs_ref[rows, :] = jnp.dot(tri, a[rows], preferred_element_type=F32,
                                  precision=lax.Precision.HIGHEST)


def _in_proj(x2d, g, w, cw, cb, dtb, a_lane, seq_len):
    t = x2d.shape[0]
    tm = min(ROW_TILE, seq_len)
    assert seq_len % tm == 0 and tm % CHUNK == 0
    hb = tm // HALO
    row = lambda n: pl.BlockSpec((tm, n), lambda i: (i, 0))
    full = lambda a: pl.BlockSpec(a.shape, lambda i: (0, 0))
    prev = pl.BlockSpec((HALO, D_MODEL), lambda i: (jnp.maximum(i * hb - 1, 0), 0))
    nxt = pl.BlockSpec((HALO, D_MODEL), lambda i: (jnp.minimum((i + 1) * hb, t // HALO - 1), 0))
    return pl.pallas_call(
        functools.partial(_in_proj_kernel, tiles_per_seq=seq_len // tm),
        grid=(t // tm,),
        in_specs=[row(D_MODEL), prev, nxt, full(g), full(w), full(cw), full(cb), full(dtb),
                  full(a_lane)],
        out_specs=[row(512), row(CONV_DIM), row(512), row(512), row(512), row(LANES), row(LANES)],
        out_shape=[jax.ShapeDtypeStruct((t, 512), BF16),
                   jax.ShapeDtypeStruct((t, CONV_DIM), BF16),
                   jax.ShapeDtypeStruct((t, 512), BF16),
                   jax.ShapeDtypeStruct((t, 512), BF16),
                   jax.ShapeDtypeStruct((t, 512), BF16),
                   jax.ShapeDtypeStruct((t, LANES), F32),
                   jax.ShapeDtypeStruct((t, LANES), F32)],
        scratch_shapes=[pltpu.VMEM((tm + 2 * HALO, CONV_DIM), F32)],
        compiler_params=_cparams(("parallel",)),
    )(x2d, x2d, x2d, g, w, cw, cb, dtb, a_lane)


def _decay_terms(dt_ref, cs_ref, a_ref, rows):
    dtv = dt_ref[0, rows, :]
    cs = cs_ref[0, rows, :]
    return dtv, cs, cs - dtv * a_ref[...], cs[CHUNK - 1:CHUNK, :]


def _pair_cols(x, h0, left):
    return jnp.where(left, x[:, h0:h0 + 1], x[:, h0 + 1:h0 + 2])


def _spread(v, sel_ref, k):
    return jnp.dot(v.astype(BF16), sel_ref[k], preferred_element_type=F32)


def _ssd_bwd_state_kernel(act_ref, dt_ref, cs_ref, sel_ref, a_ref, sin_ref, sb_ref):
    @pl.when(pl.program_id(1) == 0)
    def _():
        sb_ref[...] = jnp.zeros_like(sb_ref)

    left = lax.broadcasted_iota(I32, (1, LANES), 1) < 64
    for cc in reversed(range(SSD_CHUNKS)):
        rows = slice(CHUNK * cc, CHUNK * (cc + 1))
        sin_ref[0, cc] = sb_ref[...]
        bt = act_ref[0, rows, D_SSD:D_SSD + LANES].astype(F32).T.astype(BF16)
        dtv, _, ex, tot = _decay_terms(dt_ref, cs_ref, a_ref, rows)
        wb = _spread(jnp.exp(ex) * dtv, sel_ref, 1)
        decb = jnp.exp(tot)
        for j in range(4):
            cols = slice(LANES * j, LANES * (j + 1))
            xw = (act_ref[0, rows, cols].astype(F32) * wb[:, cols]).astype(BF16)
            st = jnp.dot(bt, xw, preferred_element_type=F32)
            sb_ref[j] = _pair_cols(decb, 8 + 2 * j, left) * sb_ref[j] + st


def _ssd_main_kernel(act_ref, dt_ref, cs_ref, z_ref, sin_ref, sel_ref, a_ref, dskip_ref, g_ref,
                     o_ref, sf_ref, y_ref):
    @pl.when(pl.program_id(1) == 0)
    def _():
        sf_ref[...] = jnp.zeros_like(sf_ref)

    lane = lax.broadcasted_iota(I32, (1, LANES), 1)
    left = lane < 64
    li = lax.broadcasted_iota(I32, (CHUNK, CHUNK), 0)
    si = lax.broadcasted_iota(I32, (CHUNK, CHUNK), 1)
    for cc in range(SSD_CHUNKS):
        rows = slice(CHUNK * cc, CHUNK * (cc + 1))
        bm_b = act_ref[0, rows, D_SSD:D_SSD + LANES]
        cm_b = act_ref[0, rows, D_SSD + LANES:]
        dtv, cs, ex, tot = _decay_terms(dt_ref, cs_ref, a_ref, rows)
        cs_t, ex_t, dt_t = cs.T, ex.T, dtv.T
        wf = _spread(jnp.exp(tot - cs) * dtv, sel_ref, 0)
        ef = _spread(jnp.exp(cs), sel_ref, 0)
        eb = _spread(jnp.exp(tot - ex), sel_ref, 1)
        decf = jnp.exp(tot)
        bt = bm_b.astype(F32).T.astype(BF16)
        zero = jnp.zeros_like(bm_b)
        for grp in range(2):
            in_grp = left if grp == 0 else jnp.logical_not(left)
            b_g = jnp.where(in_grp, bm_b, zero)
            c_g = jnp.where(in_grp, cm_b, zero)
            cb_g = lax.dot_general(cm_b, b_g, (((1,), (1,)), ((), ())),
                                   preferred_element_type=F32)
            for jj in range(2):
                j = 2 * grp + jj
                cols = slice(LANES * j, LANES * (j + 1))
                x_pb = act_ref[0, rows, cols]
                x_p = x_pb.astype(F32)
                halves = []
                for sub in range(2):
                    h = 2 * j + sub
                    dtf = dt_t[h:h + 1, :]
                    dtb = dt_t[8 + h:9 + h, :]
                    arg = jnp.where(si <= li,
                                    cs[:, h:h + 1] - cs_t[h:h + 1, :],
                                    ex_t[8 + h:9 + h, :] - ex[:, 8 + h:9 + h])
                    dsel = jnp.where(si < li, dtf, jnp.where(si > li, dtb, dtf + dtb))
                    m = (cb_g * jnp.exp(arg) * dsel).astype(BF16)
                    halves.append(jnp.dot(m, x_pb, preferred_element_type=F32))
                y = jnp.where(left, halves[0], halves[1])
                sf = sf_ref[j]
                sb = sin_ref[0, cc, j]
                y = y + jnp.dot(c_g, sf.astype(BF16), preferred_element_type=F32) * ef[:, cols]
                y = y + jnp.dot(c_g, sb.astype(BF16), preferred_element_type=F32) * eb[:, cols]
                y_ref[:, cols] = y + dskip_ref[:, cols] * x_p
                xw = (x_p * wf[:, cols]).astype(BF16)
                st = jnp.dot(bt, xw, preferred_element_type=F32)
                sf_ref[j] = _pair_cols(decf, 2 * j, left) * sf + st

        yz = y_ref[...] * _silu(z_ref[0, rows, :].astype(F32))
        o_ref[0, rows, :] = _rms(yz, g_ref[...]).astype(BF16)


def _ssd(act, dt, cs, z, a_lane, dskip, g):
    b, l, _ = act.shape
    nc = l // CHUNK
    assert nc % SSD_CHUNKS == 0
    steps = nc // SSD_CHUNKS
    span = SSD_CHUNKS * CHUNK
    rev = lambda i: steps - 1 - i
    fwd = lambda i: i
    const = lambda a: pl.BlockSpec(a.shape, lambda bi, i: (0,) * a.ndim)
    state = lambda order: pl.BlockSpec((1, SSD_CHUNKS, 4, LANES, LANES),
                                       lambda bi, i: (bi, order(i), 0, 0, 0))
    tokens = lambda n, order: pl.BlockSpec((1, span, n), lambda bi, i: (bi, order(i), 0))
    head_of_lane = jnp.arange(D_SSD)[None, :] // 64
    row = jnp.arange(LANES)[:, None]
    sel = jnp.stack([(row == head_of_lane + N_HEADS * k) for k in range(2)]).astype(BF16)

    s_in = pl.pallas_call(
        _ssd_bwd_state_kernel,
        grid=(b, steps),
        in_specs=[tokens(CONV_DIM, rev), tokens(LANES, rev), tokens(LANES, rev), const(sel),
                  const(a_lane)],
        out_specs=state(rev),
        out_shape=jax.ShapeDtypeStruct((b, nc, 4, LANES, LANES), F32),
        scratch_shapes=[pltpu.VMEM((4, LANES, LANES), F32)],
        compiler_params=_cparams(("parallel", "arbitrary")),
    )(act, dt, cs, sel, a_lane)

    return pl.pallas_call(
        _ssd_main_kernel,
        grid=(b, steps),
        in_specs=[tokens(CONV_DIM, fwd), tokens(LANES, fwd), tokens(LANES, fwd),
                  tokens(D_SSD, fwd), state(fwd), const(sel), const(a_lane), const(dskip),
                  const(g)],
        out_specs=tokens(D_SSD, fwd),
        out_shape=jax.ShapeDtypeStruct((b, l, D_SSD), BF16),
        scratch_shapes=[pltpu.VMEM((4, LANES, LANES), F32), pltpu.VMEM((CHUNK, D_SSD), F32)],
        compiler_params=_cparams(("parallel", "arbitrary")),
    )(act, dt, cs, z, s_in, sel, a_lane, dskip, g)


def _att_kernel(q_ref, k0_ref, k1_ref, k2_ref, k3_ref, v0_ref, v1_ref, v2_ref, v3_ref,
                t2_ref, g_ref, o_ref, s_ref, t_ref, p_ref, linv_ref, ov_ref, acc_ref, *, rows):
    blk = pl.program_id(1)
    lane = lax.broadcasted_iota(I32, (1, LANES), 1)
    left = lane < 64
    n_pairs = ATT_KROWS // 2
    zero_tile = jnp.zeros((GRID_W, LANES), BF16)

    def softmax_tail(par, rq, mx, kcols_of, n_tiles):
        m = jnp.max(mx, axis=1, keepdims=True)
        lsum = jnp.zeros((GRID_W, LANES), F32)
        for d in range(n_tiles):
            p = jnp.exp2(t_ref[par, rq, LANES * d:LANES * (d + 1)] - m)
            lsum = lsum + p
            p_ref[par, rq, kcols_of(d)] = p.astype(BF16)
        tot = jnp.sum(lsum, axis=1, keepdims=True)
        linv_ref[par, rq, :] = jnp.broadcast_to(1.0 / tot, (GRID_W, LANES))

    def window(i, place):
        koff = {"top": 0, "interior": -(KH // 2), "bottom": -KH}[place]
        js = {"top": max(i - KH // 2, 0), "interior": i, "bottom": min(i + KH // 2, KH)}[place]
        return koff, js, js // 2, (js + KH - 1) // 2

    def softmax_row(i, h, par, place):
        koff, js, first, last = window(i, place)
        rq = slice(GRID_W * i, GRID_W * (i + 1))
        kcols_of = lambda d: slice(LANES * (first + d), LANES * (first + d + 1))
        mx = None
        for d in range(last - first + 1):
            jl = 2 * (first + d)
            t = s_ref[par, rq, kcols_of(d)] + t2_ref[h, koff + jl - i + KH]
            if jl < js:
                t = jnp.where(left, NEG, t)
            if jl + 1 > js + KH - 1:
                t = jnp.where(left, t, NEG)
            t_ref[par, rq, LANES * d:LANES * (d + 1)] = t
            mx = t if mx is None else jnp.maximum(mx, t)
        softmax_tail(par, rq, mx, kcols_of, last - first + 1)
        for jp in range(2 * (first // 2), 2 * (last // 2) + 2):
            if jp < first or jp > last:
                p_ref[par, rq, LANES * jp:LANES * (jp + 1)] = zero_tile

    def all_heads(place):
        kt_rows = []
        for kt in range(n_pairs // 2):
            vis = [i for i in range(ATT_ROWS)
                   if window(i, place)[2] // 2 <= kt <= window(i, place)[3] // 2]
            if vis:
                kt_rows.append((kt, vis[0], vis[-1] + 1))
        k_refs = (k0_ref, k1_ref, k2_ref, k3_ref)
        v_refs = (v0_ref, v1_ref, v2_ref, v3_ref)
        for hp in range(N_HEADS // 2):
            cols = slice(LANES * hp, LANES * (hp + 1))
            o_pair = None
            for sub in range(2):
                h = 2 * hp + sub
                in_head = left if sub == 0 else jnp.logical_not(left)
                for kt, ra, rb in kt_rows:
                    keys = slice(2 * LANES * kt, 2 * LANES * (kt + 1))
                    k_tile = k_refs[kt][0, :, cols]
                    s_ref[sub, GRID_W * ra:GRID_W * rb, keys] = lax.dot_general(
                        q_ref[0, GRID_W * ra:GRID_W * rb, cols],
                        jnp.where(in_head, k_tile, jnp.zeros_like(k_tile)),
                        (((1,), (1,)), ((), ())), preferred_element_type=F32)
                for i in range(ATT_ROWS):
                    softmax_row(i, h, sub, place)
                done = 0
                for kt, ra, rb in kt_rows:
                    keys = slice(2 * LANES * kt, 2 * LANES * (kt + 1))
                    part = jnp.dot(p_ref[sub, GRID_W * ra:GRID_W * rb, keys],
                                   v_refs[kt][0, :, cols], preferred_element_type=F32)
                    old = max(min(rb, done) - ra, 0)
                    if old:
                        ov_ref[sub, GRID_W * ra:GRID_W * (ra + old), :] += part[:GRID_W * old]
                    if ra + old < rb:
                        ov_ref[sub, GRID_W * (ra + old):GRID_W * rb, :] = part[GRID_W * old:]
                    done = max(done, rb)
                o_full = ov_ref[sub] * linv_ref[sub]
                o_pair = o_full if sub == 0 else jnp.where(left, o_pair, o_full)
            acc_ref[:, cols] = o_pair

    last_blk = rows // ATT_ROWS - 1
    pl.when(blk == 0)(lambda: all_heads("top"))
    pl.when(jnp.logical_and(blk > 0, blk < last_blk))(lambda: all_heads("interior"))
    pl.when(blk == last_blk)(lambda: all_heads("bottom"))
    o_ref[0] = _rms(acc_ref[...], g_ref[...]).astype(BF16)


def _attention(q, k, v, t2, g):
    b, l, _ = q.shape
    rows = l // GRID_W
    assert rows >= ATT_KROWS and rows % ATT_ROWS == 0
    nblk = rows // ATT_ROWS
    nq = ATT_ROWS * GRID_W
    kb = 4 * GRID_W
    n_kb = l // kb

    def kspec(m):
        return pl.BlockSpec(
            (1, kb, D_ATT),
            lambda bi, i, m=m: (bi, jnp.clip(2 * i - 1, 0, n_kb - 4) + m, 0))

    const = lambda a: pl.BlockSpec(a.shape, lambda bi, i: (0,) * a.ndim)
    return pl.pallas_call(
        functools.partial(_att_kernel, rows=rows),
        grid=(b, nblk),
        in_specs=[pl.BlockSpec((1, nq, D_ATT), lambda bi, i: (bi, i, 0))]
                 + [kspec(m) for m in range(4)] + [kspec(m) for m in range(4)]
                 + [const(t2), const(g)],
        out_specs=pl.BlockSpec((1, nq, D_ATT), lambda bi, i: (bi, i, 0)),
        out_shape=jax.ShapeDtypeStruct((b, l, D_ATT), BF16),
        scratch_shapes=[pltpu.VMEM((2, nq, ATT_KROWS * GRID_W), F32),
                        pltpu.VMEM((2, nq, ATT_VISIT * LANES), F32),
                        pltpu.VMEM((2, nq, ATT_KROWS * GRID_W), BF16),
                        pltpu.VMEM((2, nq, LANES), F32),
                        pltpu.VMEM((2, nq, LANES), F32),
                        pltpu.VMEM((nq, D_ATT), F32)],
        compiler_params=_cparams(("parallel", "parallel")),
    )(q, k, k, k, k, v, v, v, v, t2, g)


def _bias_tables(rpb):
    c = jnp.arange(GRID_W)
    col_start = jnp.clip(c - KW // 2, 0, GRID_W - KW)
    col_mask = (c[None, :] >= col_start[:, None]) & (c[None, :] < col_start[:, None] + KW)
    dc = jnp.clip(c[None, :] - c[:, None] + (KW - 1), 0, 2 * KW - 2)
    t = rpb.astype(F32)[:, :, dc]
    t = jnp.where(col_mask[None, None], t * LOG2E, NEG)
    pad = jnp.full((N_HEADS, 1, GRID_W, GRID_W), NEG, F32)
    t = jnp.concatenate([pad, t, pad], axis=1)
    return jnp.concatenate([t[:, :-1], t[:, 1:]], axis=-1)


def _out_proj_kernel(x_ref, ys_ref, ya_ref, wo_ref, g_ref, wr_ref, x1_ref, u_ref, aff_ref):
    x1 = x_ref[...] \
        + jnp.dot(ys_ref[...], wo_ref[:D_SSD, :], preferred_element_type=F32) \
        + jnp.dot(ya_ref[...], wo_ref[D_SSD:, :], preferred_element_type=F32)
    x1_ref[...] = x1
    ub = _rms(x1, g_ref[...]).astype(BF16)
    u_ref[...] = ub
    logits = jnp.dot(ub, wr_ref[...], preferred_element_type=F32)
    lt = logits.T[:N_EXPERTS, :]
    e = jnp.exp(lt - jnp.max(lt, axis=0, keepdims=True))
    aff_ref[...] = e / jnp.sum(e, axis=0, keepdims=True)


def _out_proj(x2d, ys, ya, wo, g, wr):
    t = x2d.shape[0]
    tm = min(ROW_TILE, t)
    row = lambda n: pl.BlockSpec((tm, n), lambda i: (i, 0))
    full = lambda a: pl.BlockSpec(a.shape, lambda i: (0, 0))
    return pl.pallas_call(
        _out_proj_kernel,
        grid=(t // tm,),
        in_specs=[row(D_MODEL), row(D_SSD), row(D_ATT), full(wo), full(g), full(wr)],
        out_specs=[row(D_MODEL), row(D_MODEL), pl.BlockSpec((N_EXPERTS, tm), lambda i: (0, i))],
        out_shape=[jax.ShapeDtypeStruct((t, D_MODEL), F32),
                   jax.ShapeDtypeStruct((t, D_MODEL), BF16),
                   jax.ShapeDtypeStruct((N_EXPERTS, t), F32)],
        compiler_params=_cparams(("parallel",)),
    )(x2d, ys, ya, wo, g, wr)


def _route_kernel(aff_ref, sel_ref, offs_ref, cnt_ref, *, cap, nb):
    aff = aff_ref[...]
    t = aff.shape[1]
    keys = lax.bitcast_convert_type(aff, I32)
    capf = jnp.float32(cap)

    def count(mask):
        return jnp.sum(jnp.where(mask, 1.0, 0.0), axis=1, keepdims=True)

    def key_step(i, cur):
        cand = cur | lax.shift_left(jnp.int32(1), 30 - i)
        return jnp.where(count(keys >= cand) >= capf, cand, cur)

    thr = lax.fori_loop(0, 31, key_step, jnp.zeros((N_EXPERTS, 1), I32))
    gt = keys > thr
    eq = keys == thr
    need = capf - count(gt)
    idx = lax.broadcasted_iota(I32, (N_EXPERTS, t), 1)
    nbits = max(t.bit_length(), 1)

    def idx_step(i, cur):
        cand = cur | lax.shift_left(jnp.int32(1), nbits - 1 - i)
        return jnp.where(count(jnp.logical_and(eq, idx < cand)) < need, cand, cur)

    last = lax.fori_loop(0, nbits, idx_step, jnp.zeros((N_EXPERTS, 1), I32))
    sel = jnp.where(gt, 1.0, jnp.where(jnp.logical_and(eq, idx <= last), 1.0, 0.0))
    sel_ref[...] = sel

    lane = lax.broadcasted_iota(I32, (N_EXPERTS, LANES), 1)
    cnt = jnp.zeros((N_EXPERTS, LANES), F32)
    for b in range(nb):
        cb = jnp.sum(sel[:, TOK_BLOCK * b:TOK_BLOCK * (b + 1)], axis=1, keepdims=True)
        cnt = jnp.where(lane == b, cb, cnt)
    incl = cnt
    sh = 1
    while sh < LANES:
        incl = incl + jnp.where(lane >= sh, pltpu.roll(incl, sh, axis=1), 0.0)
        sh *= 2
    offs_ref[...] = (incl - cnt).astype(I32)
    cnt_ref[...] = cnt.astype(I32)


def _route(aff_t):
    t = aff_t.shape[1]
    cap = CAP_FACTOR * t // N_EXPERTS
    nb = t // TOK_BLOCK
    assert nb <= LANES
    return pl.pallas_call(
        functools.partial(_route_kernel, cap=cap, nb=nb),
        out_shape=[jax.ShapeDtypeStruct((N_EXPERTS, t), F32),
                   jax.ShapeDtypeStruct((N_EXPERTS, LANES), I32),
                   jax.ShapeDtypeStruct((N_EXPERTS, LANES), I32)],
        compiler_params=pltpu.CompilerParams(vmem_limit_bytes=VMEM_LIMIT),
    )(aff_t)


def _slot_positions(sel):
    si = lax.broadcasted_iota(I32, (TOK_BLOCK, TOK_BLOCK), 0)
    ti = lax.broadcasted_iota(I32, (TOK_BLOCK, TOK_BLOCK), 1)
    tri = jnp.where(si <= ti, 1.0, 0.0).astype(BF16)
    incl = jnp.dot(sel.astype(BF16), tri, preferred_element_type=F32)
    return jnp.where(sel > 0.0, incl - 1.0, -1e6)


def _num_chunks(cnt_ref, b):
    mx = cnt_ref[0, b]
    for e in range(1, N_EXPERTS):
        mx = jnp.maximum(mx, cnt_ref[e, b])
    return (mx + SLOT_STEP - 1) // SLOT_STEP


def _align_down(v):
    return pl.multiple_of((v // SUBLANES) * SUBLANES, SUBLANES)


def _dispatch_kernel(offs_ref, cnt_ref, u_ref, sel_ref, aff_ref, xe_hbm,
                     g_ref, stage_ref, carry_ref, sem, n_ref, *, nb, cap):
    b = pl.program_id(0)

    def copy(slot, e, row):
        return pltpu.make_async_copy(stage_ref.at[slot, e],
                                     xe_hbm.at[e, pl.ds(row, SLOT_WIN), :], sem.at[slot])

    @pl.when(b == 0)
    def _():
        n_ref[0] = 0
        carry_ref[...] = jnp.zeros_like(carry_ref)
        stage_ref[1, 0] = jnp.zeros((SLOT_WIN, 640), U32)
        for e in range(N_EXPERTS):
            pltpu.make_async_copy(stage_ref.at[1, 0], xe_hbm.at[e, pl.ds(cap, SLOT_WIN), :],
                                  sem.at[1]).start()
        for e in range(N_EXPERTS):
            copy(1, 0, 0).wait()

    posm = _slot_positions(sel_ref[...])
    aff = aff_ref[...]
    ub = u_ref[...]
    w_iota = lax.broadcasted_iota(I32, (SLOT_WIN, TOK_BLOCK), 0).astype(F32)
    head_row = lax.broadcasted_iota(I32, (SUBLANES, 640), 0)

    def chunk(k, carry):
        n = n_ref[0]
        slot = lax.rem(n, 2)
        win = []
        for e in range(N_EXPERTS):
            cnt = cnt_ref[e, b]
            done = jnp.minimum(k * SLOT_STEP, cnt)
            base = offs_ref[e, b] + done
            row = _align_down(base)
            end = base + jnp.minimum((k + 1) * SLOT_STEP, cnt) - done
            win.append((row, base - row, _align_down(end) - row))
            ge = posm[e:e + 1, :] == w_iota + (done - (base - row)).astype(F32)
            g_ref[SLOT_WIN * e:SLOT_WIN * (e + 1), :] = jnp.where(ge, 1.0, 0.0).astype(BF16)
            gate = jnp.sum(jnp.where(ge, aff[e:e + 1, :], 0.0), axis=1, keepdims=True)
            stage_ref[slot, e, :, 512:] = _bits(jnp.broadcast_to(gate, (SLOT_WIN, LANES)))
        packed = _pack_halves(jnp.dot(g_ref[...], ub, preferred_element_type=F32))
        for e in range(N_EXPERTS):
            row, skew, nxt = win[e]
            stage_ref[slot, e, :, :512] = packed[SLOT_WIN * e:SLOT_WIN * (e + 1), :]
            stage_ref[slot, e, :SUBLANES, :] = jnp.where(
                head_row < skew, carry_ref[e], stage_ref[slot, e, :SUBLANES, :])
            carry_ref[e] = stage_ref[slot, e, pl.ds(pl.multiple_of(nxt, SUBLANES), SUBLANES), :]

        @pl.when(n > 0)
        def _():
            for e in range(N_EXPERTS):
                copy(1 - slot, e, 0).wait()

        for e in range(N_EXPERTS):
            copy(slot, e, win[e][0]).start()
        n_ref[0] = n + 1
        return carry

    lax.fori_loop(0, _num_chunks(cnt_ref, b), chunk, 0)

    @pl.when(jnp.logical_and(b == nb - 1, n_ref[0] > 0))
    def _():
        slot = lax.rem(n_ref[0] - 1, 2)
        for e in range(N_EXPERTS):
            copy(slot, e, 0).wait()


def _dispatch(u, sel, aff_t, offs, cnt):
    t = u.shape[0]
    cap = CAP_FACTOR * t // N_EXPERTS
    nb = t // TOK_BLOCK
    cap_p = cap + SLOT_WIN
    grid_spec = pltpu.PrefetchScalarGridSpec(
        num_scalar_prefetch=2,
        grid=(nb,),
        in_specs=[pl.BlockSpec((TOK_BLOCK, D_MODEL), lambda b, *_: (b, 0)),
                  pl.BlockSpec((N_EXPERTS, TOK_BLOCK), lambda b, *_: (0, b)),
                  pl.BlockSpec((N_EXPERTS, TOK_BLOCK), lambda b, *_: (0, b))],
        out_specs=pl.BlockSpec(memory_space=pl.ANY),
        scratch_shapes=[pltpu.VMEM((N_EXPERTS * SLOT_WIN, TOK_BLOCK), BF16),
                        pltpu.VMEM((2, N_EXPERTS, SLOT_WIN, 640), U32),
                        pltpu.VMEM((N_EXPERTS, SUBLANES, 640), U32),
                        pltpu.SemaphoreType.DMA((2,)),
                        pltpu.SMEM((1,), I32)])
    return pl.pallas_call(
        functools.partial(_dispatch_kernel, nb=nb, cap=cap),
        grid_spec=grid_spec,
        out_shape=jax.ShapeDtypeStruct((N_EXPERTS, cap_p, 640), U32),
        compiler_params=_cparams(("arbitrary",)),
    )(offs, cnt, u, sel, aff_t)


def _ffn_kernel(xe_ref, wg_ref, wu_ref, wd_ref, o_ref):
    xp = xe_ref[0]
    x = _unpack_halves(xp[:, :512])
    gate = lax.bitcast_convert_type(xp[:, 512:513], F32)
    gp = jnp.dot(x, wg_ref[0, 0], preferred_element_type=F32)
    up = jnp.dot(x, wu_ref[0, 0], preferred_element_type=F32)
    hid = (_silu(gp) * up).astype(BF16)
    out = jnp.dot(hid, wd_ref[0, 0], preferred_element_type=F32) * gate
    o_ref[0] = _pack_halves(out.astype(BF16).astype(F32))


def _ffn(xe, wg, wu, wd, layer, cap):
    tm = min(FFN_TILE, cap)
    return pl.pallas_call(
        _ffn_kernel,
        grid=(N_EXPERTS, cap // tm),
        in_specs=[pl.BlockSpec((1, tm, 640), lambda e, j: (e, j, 0)),
                  pl.BlockSpec((1, 1, D_MODEL, D_FF), lambda e, j: (layer, e, 0, 0)),
                  pl.BlockSpec((1, 1, D_MODEL, D_FF), lambda e, j: (layer, e, 0, 0)),
                  pl.BlockSpec((1, 1, D_FF, D_MODEL), lambda e, j: (layer, e, 0, 0))],
        out_specs=pl.BlockSpec((1, tm, 512), lambda e, j: (e, j, 0)),
        out_shape=jax.ShapeDtypeStruct((N_EXPERTS, cap, 512), U32),
        compiler_params=_cparams(("parallel", "parallel")),
    )(xe, wg, wu, wd)


def _combine_kernel(offs_ref, cnt_ref, x1_ref, sel_ref, gfin_ref, oe_hbm, o_ref,
                    g_ref, stage_ref, acc_ref, sem, *, nb, cap, final_norm):
    b = pl.program_id(0)
    w_iota = lax.broadcasted_iota(I32, (SLOT_WIN, TOK_BLOCK), 0).astype(F32)

    def window(bb, k, e):
        done = jnp.minimum(k * SLOT_STEP, cnt_ref[e, bb])
        base = offs_ref[e, bb] + done
        row = jnp.minimum(_align_down(base), cap - SLOT_WIN)
        return row, done - (base - row)

    def copy(slot, e, row):
        return pltpu.make_async_copy(oe_hbm.at[e, pl.ds(row, SLOT_WIN), :],
                                     stage_ref.at[slot, pl.ds(SLOT_WIN * e, SLOT_WIN), :],
                                     sem.at[slot])

    def fetch(bb, k, slot):
        for e in range(N_EXPERTS):
            copy(slot, e, pl.multiple_of(window(bb, k, e)[0], SUBLANES)).start()

    def wait(slot):
        for e in range(N_EXPERTS):
            copy(slot, e, 0).wait()

    @pl.when(b == 0)
    def _():
        fetch(0, 0, 0)

    @pl.when(b + 1 < nb)
    def _():
        fetch(b + 1, 0, lax.rem(b + 1, 2))

    pos_all = _slot_positions(sel_ref[...])

    def expand(slot, k):
        lo = (k * SLOT_STEP).astype(F32)
        posm = jnp.where(jnp.logical_and(pos_all >= lo, pos_all < lo + SLOT_STEP), pos_all, -1e6)
        for e in range(N_EXPERTS):
            wi = w_iota + window(b, k, e)[1].astype(F32)
            g_ref[SLOT_WIN * e:SLOT_WIN * (e + 1), :] = jnp.where(posm[e:e + 1, :] == wi, 1.0, 0.0)
        gt = g_ref[...].T.astype(BF16)
        slab = _unpack_halves(stage_ref[slot])
        return jnp.dot(gt, slab, preferred_element_type=F32)

    wait(lax.rem(b, 2))
    acc_ref[...] = x1_ref[...] + expand(lax.rem(b, 2), jnp.int32(0))

    def extra(k, carry):
        fetch(b, k, 2)
        wait(2)
        acc_ref[...] += expand(2, k)
        return carry

    lax.fori_loop(1, _num_chunks(cnt_ref, b), extra, 0)
    y = acc_ref[...]
    o_ref[...] = _rms(y, gfin_ref[...]) if final_norm else y


def _combine(x1, sel, oe, offs, cnt, gfin, final_norm):
    t = x1.shape[0]
    cap = CAP_FACTOR * t // N_EXPERTS
    nb = t // TOK_BLOCK
    grid_spec = pltpu.PrefetchScalarGridSpec(
        num_scalar_prefetch=2,
        grid=(nb,),
        in_specs=[pl.BlockSpec((TOK_BLOCK, D_MODEL), lambda b, *_: (b, 0)),
                  pl.BlockSpec((N_EXPERTS, TOK_BLOCK), lambda b, *_: (0, b)),
                  pl.BlockSpec((1, D_MODEL), lambda b, *_: (0, 0)),
                  pl.BlockSpec(memory_space=pl.ANY)],
        out_specs=pl.BlockSpec((TOK_BLOCK, D_MODEL), lambda b, *_: (b, 0)),
        scratch_shapes=[pltpu.VMEM((N_EXPERTS * SLOT_WIN, TOK_BLOCK), F32),
                        pltpu.VMEM((3, N_EXPERTS * SLOT_WIN, 512), U32),
                        pltpu.VMEM((TOK_BLOCK, D_MODEL), F32),
                        pltpu.SemaphoreType.DMA((3,))])
    return pl.pallas_call(
        functools.partial(_combine_kernel, nb=nb, cap=cap, final_norm=final_norm),
        grid_spec=grid_spec,
        out_shape=jax.ShapeDtypeStruct((t, D_MODEL), F32),
        compiler_params=_cparams(("arbitrary",)),
    )(offs, cnt, x1, sel, gfin, oe)


def _prep_layer(w_in, conv_w, conv_b, dt_bias, a_log, d_skip, rpb, w_out, w_router):
    o1 = D_SSD
    o2 = o1 + CONV_DIM
    o3 = o2 + 2 * N_HEADS
    w = jnp.concatenate(
        [w_in[:, :o2], w_in[:, o3:], w_in[:, o2:o3],
         jnp.zeros((D_MODEL, LANES - 2 * N_HEADS), w_in.dtype)], axis=1).astype(BF16)
    lane_pad = lambda v: jnp.concatenate([v.reshape(-1).astype(F32),
                                          jnp.zeros((LANES - 2 * N_HEADS,), F32)])[None, :]
    return dict(
        w_in=w,
        conv_w=jnp.concatenate([conv_w.astype(F32), jnp.zeros((8 - CONV_W, CONV_DIM), F32)], 0),
        conv_b=conv_b.astype(F32)[None, :],
        dt_bias=lane_pad(dt_bias),
        a_lane=lane_pad(-jnp.exp(a_log.astype(F32))),
        d_skip=jnp.repeat(d_skip.astype(F32), 64)[None, :],
        t2=_bias_tables(rpb),
        w_out=w_out.astype(BF16),
        w_router=jnp.concatenate(
            [w_router, jnp.zeros((D_MODEL, LANES - N_EXPERTS), w_router.dtype)], 1).astype(BF16))


def _layer(x2d, b, l, p, experts, layer, g_mix, g_ssd, g_att, g_ffn, g_final, final_norm):
    t = b * l
    row = lambda v: v.astype(F32)[None, :]
    z, act, q, k, v, dt, cs = _in_proj(x2d, row(g_mix), p["w_in"], p["conv_w"], p["conv_b"],
                                       p["dt_bias"], p["a_lane"], l)
    r3 = lambda a: a.reshape(b, l, a.shape[-1])
    y_ssd = _ssd(r3(act), r3(dt), r3(cs), r3(z), p["a_lane"], p["d_skip"], row(g_ssd))
    y_att = _attention(r3(q), r3(k), r3(v), p["t2"], row(g_att))
    x1, u, aff_t = _out_proj(x2d, y_ssd.reshape(t, D_SSD), y_att.reshape(t, D_ATT),
                             p["w_out"], row(g_ffn), p["w_router"])
    sel, offs, cnt = _route(aff_t)
    xe = _dispatch(u, sel, aff_t, offs, cnt)
    oe = _ffn(xe, *experts, layer, CAP_FACTOR * t // N_EXPERTS)
    return _combine(x1, sel, oe, offs, cnt, row(g_final), final_norm)


def _trunk(x, layers, experts, norm_mix_g, ssd_norm_g, attn_norm_g, norm_ffn_g, norm_final_g):
    b, l, _ = x.shape
    x2d = x.reshape(b * l, D_MODEL)
    depth = len(layers)
    for i, p in enumerate(layers):
        x2d = _layer(x2d, b, l, p, experts, i, norm_mix_g[i], ssd_norm_g[i], attn_norm_g[i],
                     norm_ffn_g[i], norm_final_g, i == depth - 1)
    return x2d.reshape(b, l, D_MODEL)


def kernel(x_prompt, x_sample, norm_mix_g, w_in, conv_w, conv_b, dt_bias, a_log, d_skip, ssd_norm_g, attn_norm_g, rpb, w_out, norm_ffn_g, w_router, w_gate, w_up, w_down, norm_final_g):
    depth = w_in.shape[0]
    layers = [_prep_layer(w_in[i], conv_w[i], conv_b[i], dt_bias[i], a_log[i], d_skip[i], rpb[i],
                          w_out[i], w_router[i])
              for i in range(depth)]
    experts = (w_gate.astype(BF16), w_up.astype(BF16), w_down.astype(BF16))
    run = functools.partial(_trunk, layers=layers, experts=experts,
                            norm_mix_g=norm_mix_g, ssd_norm_g=ssd_norm_g,
                            attn_norm_g=attn_norm_g, norm_ffn_g=norm_ffn_g,
                            norm_final_g=norm_final_g)
    return (run(x_prompt), run(x_sample))
```
